```python
import jax, jax.numpy as jnp
from jax import lax
import numpy as np

D_MODEL = 2048
BATCH = 4
SEQ = 4096
DEPTH = 4

N_MIXERS = 2
N_MLSTM_LAYERS = (DEPTH + 1) // 2
N_GMLP_LAYERS = DEPTH // 2
MLSTM_HEADS = 8
MLSTM_V_DIM = D_MODEL // MLSTM_HEADS
MLSTM_QK_DIM = MLSTM_V_DIM // 2
MLSTM_CHUNK = 128
MLSTM_QK_W = MLSTM_HEADS * MLSTM_QK_DIM
MLSTM_V_W = MLSTM_HEADS * MLSTM_V_DIM
MLSTM_N_GATES = 4 * MLSTM_HEADS
MLSTM_IN_COLS = 2 * MLSTM_QK_W + 2 * MLSTM_V_W + MLSTM_N_GATES
F_BIAS_LO = 3.0
F_BIAS_HI = 6.0
GMLP_WIDTH = D_MODEL
GMLP_GROUPS = 8
GMLP_GROUP_DIM = GMLP_WIDTH // GMLP_GROUPS
GMLP_CHUNK = 128
N_EXPERTS = 32
TOP_K = 4
EXPERT_FF = D_MODEL // 4
SWIGLU_LIMIT = 7.0
SWIGLU_ALPHA = 1.702
MOE_BLOCK = 128
DEEPNORM_ALPHA = (2 * DEPTH) ** 0.25
DEEPNORM_BETA = (8 * DEPTH) ** -0.25
LN_EPS = 1e-5

kernel_name = 'bidir_mlstm_gmlp_moe_deepnorm'


def layer_norm(x, g, b):
    xf = x.astype(jnp.float32)
    mu = jnp.mean(xf, axis=-1, keepdims=True)
    xc = xf - mu
    var = jnp.mean(xc * xc, axis=-1, keepdims=True)
    y = xc * lax.rsqrt(var + LN_EPS)
    return (y * g.astype(jnp.float32) + b.astype(jnp.float32)).astype(x.dtype)


def mlstm_scan(q, k, v, ig, lf):
    B, H, S, dk = q.shape
    dv = v.shape[-1]
    L = MLSTM_CHUNK
    nc = S // L

    def to_chunks(a):
        return jnp.moveaxis(a.reshape((B, H, nc, L) + a.shape[3:]), 2, 0)

    tri = jnp.tril(jnp.ones((L, L), dtype=bool))

    def step(carry, inp):
        C, n, m = carry
        qc, kc, vc, ic, fc = inp
        b = jnp.cumsum(fc, axis=-1)
        dmat = jnp.where(tri, b[..., :, None] - b[..., None, :] + ic[..., None, :], -jnp.inf)
        inter = b + m[..., None]
        m_t = jnp.maximum(inter, jnp.max(dmat, axis=-1))
        s = jnp.einsum('bhtd,bhsd->bhts', qc, kc) * jnp.exp(dmat - m_t[..., None])
        inter_scale = jnp.exp(inter - m_t)
        num = jnp.einsum('bhts,bhsv->bhtv', s, vc) + inter_scale[..., None] * jnp.einsum('bhtd,bhdv->bhtv', qc, C)
        den = jnp.sum(s, axis=-1) + inter_scale * jnp.einsum('bhtd,bhd->bht', qc, n)
        h = num / jnp.maximum(jnp.abs(den), jnp.exp(-m_t))[..., None]
        b_last = b[..., -1]
        g = b_last[..., None] - b + ic
        m_new = jnp.maximum(b_last + m, jnp.max(g, axis=-1))
        decay = jnp.exp(b_last + m - m_new)
        wk = jnp.exp(g - m_new[..., None])
        C_new = decay[..., None, None] * C + jnp.einsum('bhs,bhsd,bhsv->bhdv', wk, kc, vc)
        n_new = decay[..., None] * n + jnp.einsum('bhs,bhsd->bhd', wk, kc)
        return (C_new, n_new, m_new), h

    init = (jnp.zeros((B, H, dk, dv), jnp.float32),
            jnp.zeros((B, H, dk), jnp.float32),
            jnp.zeros((B, H), jnp.float32))
    _, hs = lax.scan(step, init, (to_chunks(q), to_chunks(k), to_chunks(v), to_chunks(ig), to_chunks(lf)))
    return jnp.moveaxis(hs, 0, 2).reshape(B, H, S, dv)


def mlstm_mixer(x, w_in, b_gates, norm_g, w_out):
    B, S, _ = x.shape
    H = MLSTM_HEADS
    proj = x @ w_in
    q, k, v, o, gates = jnp.split(
        proj, [MLSTM_QK_W, 2 * MLSTM_QK_W, 2 * MLSTM_QK_W + MLSTM_V_W, 2 * MLSTM_QK_W + 2 * MLSTM_V_W], axis=-1)
    gates = gates.astype(jnp.float32) + b_gates.astype(jnp.float32)
    gates = jnp.transpose(gates.reshape(B, S, 4, H), (2, 0, 3, 1))
    i_fw, f_fw, i_bw, f_bw = gates[0], gates[1], gates[2], gates[3]

    def heads(a, d):
        return jnp.transpose(a.reshape(B, S, H, d), (0, 2, 1, 3)).astype(jnp.float32)

    qh = heads(q, MLSTM_QK_DIM) * (MLSTM_QK_DIM ** -0.5)
    kh = heads(k, MLSTM_QK_DIM)
    vh = heads(v, MLSTM_V_DIM)
    h_fw = mlstm_scan(qh, kh, vh, i_fw, jax.nn.log_sigmoid(f_fw))
    flip = lambda a: jnp.flip(a, axis=2)
    h_bw = flip(mlstm_scan(flip(qh), flip(kh), flip(vh), flip(i_bw), flip(jax.nn.log_sigmoid(f_bw))))
    h = h_fw + h_bw
    mu = jnp.mean(h, axis=-1, keepdims=True)
    hc = h - mu
    h = hc * lax.rsqrt(jnp.mean(hc * hc, axis=-1, keepdims=True) + LN_EPS)
    h = jnp.transpose(h, (0, 2, 1, 3)).reshape(B, S, MLSTM_V_W) * norm_g.astype(jnp.float32)
    h = (jax.nn.sigmoid(o.astype(jnp.float32)) * h).astype(x.dtype)
    return h @ w_out


def gmlp_mixer(x, w_in, b_in, vnorm_g, vnorm_b, w_s, b_s, w_out):
    B, S, _ = x.shape
    z = jax.nn.gelu(x @ w_in + b_in, approximate=False)
    u, v = jnp.split(z, 2, axis=-1)
    v = layer_norm(v, vnorm_g, vnorm_b)
    nch = S // GMLP_CHUNK
    vc = v.reshape(B, nch, GMLP_CHUNK, GMLP_GROUPS, GMLP_GROUP_DIM)
    s = jnp.einsum('gts,bcsgd->bctgd', w_s, vc) + jnp.transpose(b_s)[:, :, None]
    y = u * s.reshape(B, S, GMLP_WIDTH)
    return y @ w_out


def moe(x, router_w, router_b, w_gu, b_gu, w_dn, b_dn):
    B, S, D = x.shape
    T = B * S
    A = T * TOP_K
    xf = x.reshape(T, D)
    logits = (xf @ router_w + router_b).astype(jnp.float32)
    top_v, top_e = lax.top_k(logits, TOP_K)
    gate = jax.nn.softmax(top_v, axis=-1).astype(x.dtype)
    e_flat = top_e.reshape(A).astype(jnp.int32)
    order = jnp.argsort(e_flat)
    e_sorted = e_flat[order]
    tok_sorted = (order // TOP_K).astype(jnp.int32)
    w_sorted = gate.reshape(A)[order]
    counts = jnp.bincount(e_flat, length=N_EXPERTS).astype(jnp.int32)
    start = jnp.cumsum(counts) - counts
    rank = jnp.arange(A, dtype=jnp.int32) - start[e_sorted]
    padded = (counts + MOE_BLOCK - 1) // MOE_BLOCK * MOE_BLOCK
    pend = jnp.cumsum(padded)
    pstart = pend - padded
    dest = pstart[e_sorted] + rank
    n_slots = A + N_EXPERTS * MOE_BLOCK
    n_blocks = n_slots // MOE_BLOCK
    slot_tok = jnp.zeros((n_slots,), jnp.int32).at[dest].set(tok_sorted)
    slot_w = jnp.zeros((n_slots,), x.dtype).at[dest].set(w_sorted)
    block_start = jnp.arange(n_blocks, dtype=jnp.int32) * MOE_BLOCK
    block_e = jnp.minimum(jnp.searchsorted(pend, block_start, side='right'), N_EXPERTS - 1).astype(jnp.int32)

    def expert_block(args):
        toks, e = args
        xb = xf[toks]
        hgu = xb @ w_gu[e] + b_gu[e]
        g = jnp.minimum(hgu[:, :EXPERT_FF], SWIGLU_LIMIT)
        u = jnp.clip(hgu[:, EXPERT_FF:], -SWIGLU_LIMIT, SWIGLU_LIMIT)
        hact = (u + 1) * (g * jax.nn.sigmoid(g * SWIGLU_ALPHA))
        return hact @ w_dn[e] + b_dn[e]

    yb = lax.map(expert_block, (slot_tok.reshape(n_blocks, MOE_BLOCK), block_e))
    y = yb.reshape(n_slots, D) * slot_w[:, None]
    out = jnp.zeros((T, D), x.dtype).at[slot_tok].add(y)
    return out.reshape(B, S, D)


def setup_inputs(seed: int = 0) -> dict:
    key = jax.random.key(seed)
    ks = jax.random.split(key, 21)
    f32 = jnp.float32
    nrm = lambda k, shape, scale: jax.random.normal(k, shape, f32) * scale
    H = MLSTM_HEADS
    fb = jnp.linspace(F_BIAS_LO, F_BIAS_HI, H, dtype=f32)
    zb = jnp.zeros((H,), f32)
    gate_base = jnp.concatenate([zb, fb, zb, fb])
    return {
        'x': nrm(ks[0], (BATCH, SEQ, D_MODEL), 1.0),
        'mlstm_w_in': nrm(ks[1], (N_MLSTM_LAYERS, D_MODEL, MLSTM_IN_COLS), D_MODEL ** -0.5),
        'mlstm_b_gates': gate_base + nrm(ks[2], (N_MLSTM_LAYERS, MLSTM_N_GATES), 0.1),
        'mlstm_norm_g': 1.0 + nrm(ks[3], (N_MLSTM_LAYERS, MLSTM_V_W), 0.02),
        'mlstm_w_out': nrm(ks[4], (N_MLSTM_LAYERS, MLSTM_V_W, D_MODEL), MLSTM_V_W ** -0.5 * DEEPNORM_BETA),
        'gmlp_w_in': nrm(ks[5], (N_GMLP_LAYERS, D_MODEL, 2 * GMLP_WIDTH), D_MODEL ** -0.5),
        'gmlp_b_in': nrm(ks[6], (N_GMLP_LAYERS, 2 * GMLP_WIDTH), 0.02),
        'gmlp_vnorm_g': 1.0 + nrm(ks[7], (N_GMLP_LAYERS, GMLP_WIDTH), 0.02),
        'gmlp_vnorm_b': nrm(ks[8], (N_GMLP_LAYERS, GMLP_WIDTH), 0.02),
        'gmlp_w_s': nrm(ks[9], (N_GMLP_LAYERS, GMLP_GROUPS, GMLP_CHUNK, GMLP_CHUNK), GMLP_CHUNK ** -0.5),
        'gmlp_b_s': 1.0 + nrm(ks[10], (N_GMLP_LAYERS, GMLP_GROUPS, GMLP_CHUNK), 0.02),
        'gmlp_w_out': nrm(ks[11], (N_GMLP_LAYERS, GMLP_WIDTH, D_MODEL), GMLP_WIDTH ** -0.5 * DEEPNORM_BETA),
        'router_w': nrm(ks[12], (DEPTH, D_MODEL, N_EXPERTS), D_MODEL ** -0.5),
        'router_b': nrm(ks[13], (DEPTH, N_EXPERTS), 0.01),
        'expert_w_gu': nrm(ks[14], (DEPTH, N_EXPERTS, D_MODEL, 2 * EXPERT_FF), D_MODEL ** -0.5),
        'expert_b_gu': nrm(ks[15], (DEPTH, N_EXPERTS, 2 * EXPERT_FF), 0.02),
        'expert_w_down': nrm(ks[16], (DEPTH, N_EXPERTS, EXPERT_FF, D_MODEL), EXPERT_FF ** -0.5 * DEEPNORM_BETA),
        'expert_b_down': nrm(ks[17], (DEPTH, N_EXPERTS, D_MODEL), 0.02),
        'ln_g': 1.0 + nrm(ks[18], (DEPTH, 2, D_MODEL), 0.02),
        'ln_b': nrm(ks[19], (DEPTH, 2, D_MODEL), 0.02),
    }


def reference(x, mlstm_w_in, mlstm_b_gates, mlstm_norm_g, mlstm_w_out,
              gmlp_w_in, gmlp_b_in, gmlp_vnorm_g, gmlp_vnorm_b, gmlp_w_s, gmlp_b_s, gmlp_w_out,
              router_w, router_b, expert_w_gu, expert_b_gu, expert_w_down, expert_b_down,
              ln_g, ln_b):
    for layer in range(DEPTH):
        j = layer // N_MIXERS
        if layer % N_MIXERS == 0:
            mix = mlstm_mixer(x, mlstm_w_in[j], mlstm_b_gates[j], mlstm_norm_g[j], mlstm_w_out[j])
        else:
            mix = gmlp_mixer(x, gmlp_w_in[j], gmlp_b_in[j], gmlp_vnorm_g[j], gmlp_vnorm_b[j],
                             gmlp_w_s[j], gmlp_b_s[j], gmlp_w_out[j])
        x = layer_norm(DEEPNORM_ALPHA * x + mix, ln_g[layer, 0], ln_b[layer, 0])
        ffn = moe(x, router_w[layer], router_b[layer], expert_w_gu[layer], expert_b_gu[layer],
                  expert_w_down[layer], expert_b_down[layer])
        x = layer_norm(DEEPNORM_ALPHA * x + ffn, ln_g[layer, 1], ln_b[layer, 1])
    return x
```

```python
import functools

import jax
import jax.numpy as jnp
from jax import lax
from jax.experimental import pallas as pl
from jax.experimental.pallas import tpu as pltpu

MLSTM_HEADS = 8
CHUNK = 128
GMLP_GROUPS = 8
TOP_K = 4
MOE_BLOCK = 128
SWIGLU_LIMIT = 7.0
SWIGLU_ALPHA = 1.702
LN_EPS = 1e-5
LANES = 128
VMEM_LIMIT = 56 * 1024 * 1024

F32 = jnp.float32
BF16 = jnp.bfloat16


def _params(*sem):
    return pltpu.CompilerParams(dimension_semantics=sem, vmem_limit_bytes=VMEM_LIMIT)


def _gelu(x):
    return 0.5 * x * (1.0 + lax.erf(x * (2.0 ** -0.5)))


def _layer_norm(z, g, b):
    mu = jnp.mean(z, axis=-1, keepdims=True)
    zc = z - mu
    var = jnp.mean(zc * zc, axis=-1, keepdims=True)
    return zc * lax.rsqrt(var + LN_EPS) * g + b


def _mm_kernel(x_ref, w_ref, b_ref, o_ref, *, act):
    acc = jnp.dot(x_ref[...], w_ref[...], preferred_element_type=F32)
    acc = acc + b_ref[...]
    if act == "gelu":
        acc = _gelu(acc)
    o_ref[...] = acc.astype(o_ref.dtype)


def _matmul(x, w, b, *, act=None, out_dtype=BF16, tm=1024, tn=1024):
    m, k = x.shape
    n = w.shape[1]
    tm, tn = min(tm, m), min(tn, n)
    assert m % tm == 0 and n % tn == 0
    return pl.pallas_call(
        functools.partial(_mm_kernel, act=act),
        grid=(m // tm, n // tn),
        in_specs=[
            pl.BlockSpec((tm, k), lambda i, j: (i, 0)),
            pl.BlockSpec((k, tn), lambda i, j: (0, j)),
            pl.BlockSpec((1, tn), lambda i, j: (0, j)),
        ],
        out_specs=pl.BlockSpec((tm, tn), lambda i, j: (i, j)),
        out_shape=jax.ShapeDtypeStruct((m, n), out_dtype),
        compiler_params=_params("parallel", "parallel"),
        name="matmul_bias_act",
    )(x, w, b.reshape(1, n).astype(F32))


def _mm_ln_kernel(*refs, alpha, residual, gelu, two_out):
    if residual:
        h_ref, w_ref, res_ref, g_ref, b_ref = refs[:5]
        outs = refs[5:]
    else:
        h_ref, w_ref, bias_ref, g_ref, b_ref = refs[:5]
        outs = refs[5:]
    acc = jnp.dot(h_ref[...], w_ref[...], preferred_element_type=F32)
    if residual:
        z = alpha * res_ref[...] + acc
    else:
        z = acc + bias_ref[...]
    if gelu:
        z = _gelu(z)
    y = _layer_norm(z, g_ref[...], b_ref[...])
    if two_out:
        outs[0][...] = y
        outs[1][...] = y.astype(BF16)
    else:
        outs[0][...] = y.astype(outs[0].dtype)


def _matmul_res_ln(h, w, res, g, b, alpha, *, tm=256):
    m, k = h.shape
    n = w.shape[1]
    tm = min(tm, m)
    assert m % tm == 0
    row = lambda i: (i, 0)
    fix = lambda i: (0, 0)
    return pl.pallas_call(
        functools.partial(_mm_ln_kernel, alpha=alpha, residual=True, gelu=False, two_out=True),
        grid=(m // tm,),
        in_specs=[
            pl.BlockSpec((tm, k), row),
            pl.BlockSpec((k, n), fix),
            pl.BlockSpec((tm, n), row),
            pl.BlockSpec((1, n), fix),
            pl.BlockSpec((1, n), fix),
        ],
        out_specs=[pl.BlockSpec((tm, n), row), pl.BlockSpec((tm, n), row)],
        out_shape=[jax.ShapeDtypeStruct((m, n), F32), jax.ShapeDtypeStruct((m, n), BF16)],
        compiler_params=_params("parallel"),
        name="matmul_residual_ln",
    )(h, w, res, g.reshape(1, n), b.reshape(1, n))


def _matmul_gelu_ln(h, w, bias, g, b, *, tm=256):
    m, k = h.shape
    n = w.shape[1]
    tm = min(tm, m)
    assert m % tm == 0
    row = lambda i: (i, 0)
    fix = lambda i: (0, 0)
    return pl.pallas_call(
        functools.partial(_mm_ln_kernel, alpha=None, residual=False, gelu=True, two_out=False),
        grid=(m // tm,),
        in_specs=[
            pl.BlockSpec((tm, k), row),
            pl.BlockSpec((k, n), fix),
            pl.BlockSpec((1, n), fix),
            pl.BlockSpec((1, n), fix),
            pl.BlockSpec((1, n), fix),
        ],
        out_specs=pl.BlockSpec((tm, n), row),
        out_shape=jax.ShapeDtypeStruct((m, n), BF16),
        compiler_params=_params("parallel"),
        name="matmul_gelu_ln",
    )(h, w, bias.reshape(1, n), g.reshape(1, n), b.reshape(1, n))


def _gates_kernel(w_ref, x_ref, b_ref, o_ref, *, heads):
    g = lax.dot_general(w_ref[...], x_ref[...], (((1,), (1,)), ((), ())),
                        preferred_element_type=F32)
    g = g + b_ref[...]
    r = lax.broadcasted_iota(jnp.int32, g.shape, 0)
    is_forget = ((r >= heads) & (r < 2 * heads)) | (r >= 3 * heads)
    log_sig = jnp.minimum(g, 0.0) - jnp.log1p(jnp.exp(-jnp.abs(g)))
    o_ref[...] = jnp.where(is_forget, log_sig, g)


def _mlstm_gates(x_bf, w_gates_t, b_gates, *, tm=1024):
    t, k = x_bf.shape
    ng = w_gates_t.shape[0]
    tm = min(tm, t)
    assert t % tm == 0
    return pl.pallas_call(
        functools.partial(_gates_kernel, heads=ng // 4),
        grid=(t // tm,),
        in_specs=[
            pl.BlockSpec((ng, k), lambda i: (0, 0)),
            pl.BlockSpec((tm, k), lambda i: (i, 0)),
            pl.BlockSpec((ng, 1), lambda i: (0, 0)),
        ],
        out_specs=pl.BlockSpec((ng, tm), lambda i: (0, i)),
        out_shape=jax.ShapeDtypeStruct((ng, t), F32),
        compiler_params=_params("parallel"),
        name="mlstm_gates",
    )(w_gates_t, x_bf, b_gates.reshape(ng, 1).astype(F32))


def _mlstm_kernel(q_ref, k_ref, v_ref, o_ref, gt_ref, ng_ref, out_ref, hfw_ref, c_ref, *, nc, scale):
    L = CHUNK
    dk = q_ref.shape[1]
    row_i = lax.broadcasted_iota(jnp.int32, (L, L), 0)
    col_i = lax.broadcasted_iota(jnp.int32, (L, L), 1)
    eye = row_i == col_i

    def chunk_step(c, carry, reverse):
        n, m = carry
        off = pl.multiple_of(c * L, L)
        qc = q_ref[pl.ds(off, L), :]
        kc = k_ref[pl.ds(off, L), :]
        vc = v_ref[pl.ds(off, L), :]
        rows = gt_ref[0, 0, c]
        r0 = 2 if reverse else 0
        i_row = rows[r0:r0 + 1, :]
        f_row = rows[r0 + 1:r0 + 2, :]
        mask = (col_i >= row_i) if reverse else (col_i <= row_i)
        b_col = jnp.sum(jnp.where(mask, f_row, 0.0), axis=1, keepdims=True)
        b_row = jnp.sum(jnp.where(eye, b_col, 0.0), axis=0, keepdims=True)
        i_col = jnp.sum(jnp.where(eye, i_row, 0.0), axis=1, keepdims=True)
        dmat = jnp.where(mask, b_col - b_row + i_row, -jnp.inf)
        inter = b_col + m
        m_t = jnp.maximum(inter, jnp.max(dmat, axis=1, keepdims=True))
        p = jnp.exp(dmat - m_t)
        sqk = lax.dot_general(qc, kc, (((1,), (1,)), ((), ())), preferred_element_type=F32)
        s = sqk * (scale * p)
        isc = jnp.exp(inter - m_t)
        qn = jnp.sum(qc.astype(F32) * n, axis=1, keepdims=True)
        den = jnp.sum(s, axis=1, keepdims=True) + isc * (scale * qn)
        c_old = c_ref[...]
        num = jnp.dot(s.astype(BF16), vc, preferred_element_type=F32)
        num = num + (isc * scale) * jnp.dot(qc, c_old.astype(BF16), preferred_element_type=F32)
        h = num / jnp.maximum(jnp.abs(den), jnp.exp(-m_t))
        b_last = jnp.sum(f_row, axis=1, keepdims=True)
        g_col = b_last - b_col + i_col
        m_new = jnp.maximum(b_last + m, jnp.max(g_col, axis=0, keepdims=True))
        decay = jnp.exp(b_last + m - m_new)
        wk = jnp.exp(g_col - m_new)
        kw = kc.astype(F32) * wk
        c_ref[...] = decay * c_old + lax.dot_general(
            kw.astype(BF16), vc, (((0,), (0,)), ((), ())), preferred_element_type=F32)
        n_new = decay * n + jnp.sum(kw, axis=0, keepdims=True)
        return h, off, (n_new, m_new)

    init = (jnp.zeros((1, dk), F32), jnp.zeros((1, 1), F32))

    def fw_body(c, carry):
        h, off, carry = chunk_step(c, carry, False)
        hfw_ref[pl.ds(off, L), :] = h
        return carry

    c_ref[...] = jnp.zeros_like(c_ref)
    lax.fori_loop(0, nc, fw_body, init)

    def bw_body(i, carry):
        h, off, carry = chunk_step(nc - 1 - i, carry, True)
        tot = hfw_ref[pl.ds(off, L), :] + h
        mu = jnp.mean(tot, axis=1, keepdims=True)
        hc = tot - mu
        var = jnp.mean(hc * hc, axis=1, keepdims=True)
        hn = hc * lax.rsqrt(var + LN_EPS) * ng_ref[...]
        og = jax.nn.sigmoid(o_ref[pl.ds(off, L), :].astype(F32))
        out_ref[pl.ds(off, L), :] = (og * hn).astype(out_ref.dtype)
        return carry

    c_ref[...] = jnp.zeros_like(c_ref)
    lax.fori_loop(0, nc, bw_body, init)


def _mlstm_scan(proj, gates_t, norm_g, *, batch, seq):
    H = MLSTM_HEADS
    L = CHUNK
    v_w = norm_g.shape[0]
    dv = v_w // H
    dk = (proj.shape[1] - 2 * v_w) // (2 * H)
    nc = seq // L
    assert seq % L == 0 and dv % LANES == 0 and dk % LANES == 0
    gt = gates_t.reshape(4, H, batch, nc, L).transpose(2, 1, 3, 0, 4)
    gt = jnp.pad(gt, ((0, 0), (0, 0), (0, 0), (0, 4), (0, 0)))
    k0 = H
    v0 = 2 * H * dk // dv
    o0 = v0 + H
    return pl.pallas_call(
        functools.partial(_mlstm_kernel, nc=nc, scale=float(dk) ** -0.5),
        grid=(batch, H),
        in_specs=[
            pl.BlockSpec((seq, dk), lambda b, h: (b, h)),
            pl.BlockSpec((seq, dk), lambda b, h: (b, k0 + h)),
            pl.BlockSpec((seq, dv), lambda b, h: (b, v0 + h)),
            pl.BlockSpec((seq, dv), lambda b, h: (b, o0 + h)),
            pl.BlockSpec((1, 1, nc, 8, L), lambda b, h: (b, h, 0, 0, 0)),
            pl.BlockSpec((1, dv), lambda b, h: (0, h)),
        ],
        out_specs=pl.BlockSpec((seq, dv), lambda b, h: (b, h)),
        out_shape=jax.ShapeDtypeStruct((batch * seq, v_w), BF16),
        scratch_shapes=[pltpu.VMEM((seq, dv), F32), pltpu.VMEM((dk, dv), F32)],
        compiler_params=_params("parallel", "parallel"),
        name="mlstm_scan",
    )(proj, proj, proj, proj, gt, norm_g.reshape(1, v_w).astype(F32))


def _spatial_kernel(ws_ref, bs_ref, v_ref, u_ref, y_ref, *, groups):
    gd = v_ref.shape[1] // groups
    for g in range(groups):
        cols = slice(g * gd, (g + 1) * gd)
        s = jnp.dot(ws_ref[g], v_ref[:, cols], preferred_element_type=F32) + bs_ref[g]
        y_ref[:, cols] = (u_ref[:, cols].astype(F32) * s).astype(y_ref.dtype)


def _gmlp_spatial(u, vn, w_s, b_s):
    t, width = u.shape
    groups, L, _ = w_s.shape
    blk = pl.BlockSpec((L, width), lambda c: (c, 0))
    return pl.pallas_call(
        functools.partial(_spatial_kernel, groups=groups),
        grid=(t // L,),
        in_specs=[
            pl.BlockSpec((groups, L, L), lambda c: (0, 0, 0)),
            pl.BlockSpec((groups, L, 1), lambda c: (0, 0, 0)),
            blk, blk,
        ],
        out_specs=blk,
        out_shape=jax.ShapeDtypeStruct((t, width), BF16),
        compiler_params=_params("parallel"),
        name="gmlp_spatial",
    )(w_s.astype(BF16), b_s.reshape(groups, L, 1).astype(F32), vn, u)


def _router_kernel(x_ref, w_ref, b_ref, gate_ref, idx_ref, *, n_experts):
    logits = jnp.dot(x_ref[...], w_ref[...], preferred_element_type=F32,
                     precision=lax.Precision.HIGHEST) + b_ref[...]
    lane = lax.broadcasted_iota(jnp.int32, logits.shape, 1)
    cur = jnp.where(lane < n_experts, logits, -jnp.inf)
    vals, idxs = [], []
    for _ in range(TOP_K):
        mx = jnp.max(cur, axis=1, keepdims=True)
        ix = jnp.min(jnp.where(cur == mx, lane, LANES), axis=1, keepdims=True)
        vals.append(mx)
        idxs.append(ix)
        cur = jnp.where(lane == ix, -jnp.inf, cur)
    exps = [jnp.exp(v - vals[0]) for v in vals]
    tot = exps[0]
    for e in exps[1:]:
        tot = tot + e
    gate = jnp.zeros(logits.shape, F32)
    idx = jnp.zeros(logits.shape, jnp.int32)
    for k in range(TOP_K):
        gate = jnp.where(lane == k, exps[k] / tot, gate)
        idx = jnp.where(lane == k, idxs[k], idx)
    gate_ref[...] = gate
    idx_ref[...] = idx


def _router(x, rw, rb, *, tm=512):
    t, d = x.shape
    e = rw.shape[1]
    assert e <= LANES
    tm = min(tm, t)
    w_pad = jnp.pad(rw.astype(F32), ((0, 0), (0, LANES - e)))
    b_pad = jnp.pad(rb.astype(F32), (0, LANES - e)).reshape(1, LANES)
    return pl.pallas_call(
        functools.partial(_router_kernel, n_experts=e),
        grid=(t // tm,),
        in_specs=[
            pl.BlockSpec((tm, d), lambda i: (i, 0)),
            pl.BlockSpec((d, LANES), lambda i: (0, 0)),
            pl.BlockSpec((1, LANES), lambda i: (0, 0)),
        ],
        out_specs=[pl.BlockSpec((tm, LANES), lambda i: (i, 0))] * 2,
        out_shape=[jax.ShapeDtypeStruct((t, LANES), F32), jax.ShapeDtypeStruct((t, LANES), jnp.int32)],
        compiler_params=_params("parallel"),
        name="moe_router",
    )(x, w_pad, b_pad)


RANK_BLOCK = 512


def _rank_kernel(e_ref, rank_ref, cnt_ref, carry_ref):
    @pl.when(pl.program_id(0) == 0)
    def _():
        carry_ref[...] = jnp.zeros_like(carry_ref)

    e_row = e_ref[0]
    ex = lax.broadcasted_iota(jnp.int32, (LANES, RANK_BLOCK), 0)
    onehot = e_row == ex
    a_i = lax.broadcasted_iota(jnp.int32, (RANK_BLOCK, RANK_BLOCK), 0)
    b_i = lax.broadcasted_iota(jnp.int32, (RANK_BLOCK, RANK_BLOCK), 1)
    before = (a_i < b_i).astype(BF16)
    within = jnp.dot(onehot.astype(BF16), before, preferred_element_type=F32)
    carry = carry_ref[...][:, :1]
    rank = jnp.sum(jnp.where(onehot, within + carry, 0.0), axis=0, keepdims=True)
    rank_ref[0] = rank.astype(jnp.int32)
    new = carry + jnp.sum(onehot.astype(F32), axis=1, keepdims=True)
    carry_ref[...] = jnp.broadcast_to(new, carry_ref.shape)
    cnt_ref[...] = jnp.broadcast_to(new, cnt_ref.shape).astype(jnp.int32)


def _expert_ranks(e_flat, n_experts):
    a = e_flat.shape[0]
    assert a % RANK_BLOCK == 0 and n_experts <= LANES
    nb = a // RANK_BLOCK
    rank, cnt = pl.pallas_call(
        _rank_kernel,
        grid=(nb,),
        in_specs=[pl.BlockSpec((1, 1, RANK_BLOCK), lambda i: (i, 0, 0))],
        out_specs=[pl.BlockSpec((1, 1, RANK_BLOCK), lambda i: (i, 0, 0)),
                   pl.BlockSpec((LANES, LANES), lambda i: (0, 0))],
        out_shape=[jax.ShapeDtypeStruct((nb, 1, RANK_BLOCK), jnp.int32),
                   jax.ShapeDtypeStruct((LANES, LANES), jnp.int32)],
        scratch_shapes=[pltpu.VMEM((LANES, LANES), F32)],
        compiler_params=_params("arbitrary"),
        name="moe_rank",
    )(e_flat.reshape(nb, 1, RANK_BLOCK))
    return rank.reshape(a), cnt[:n_experts, 0]


def _expert_kernel(be_ref, nused_ref, nvalid_ref, dst_ref, x_hbm, wgu_ref, bgu_ref, wdn_ref, bdn_ref,
                   out_hbm, xbuf, ybuf, gsem, ssem, *, n_tokens, ff):
    j = pl.program_id(0)
    nb = pl.num_programs(0)
    slot = lax.rem(j, 2)
    nused = nused_ref[0]
    R = MOE_BLOCK

    def start_gather(blk, sl):
        for r in range(R):
            tok = dst_ref[blk, r] & (n_tokens - 1)
            pltpu.make_async_copy(x_hbm.at[pl.ds(tok, 1)], xbuf.at[sl, pl.ds(r, 1)], gsem.at[sl]).start()

    def wait_gather(sl):
        pltpu.make_async_copy(x_hbm.at[pl.ds(0, R)], xbuf.at[sl], gsem.at[sl]).wait()

    def scatter_row(blk, sl, r):
        return pltpu.make_async_copy(ybuf.at[sl, pl.ds(r, 1)], out_hbm.at[pl.ds(dst_ref[blk, r], 1)],
                                     ssem.at[sl])

    def start_scatter(blk, sl):
        nv = nvalid_ref[blk]

        @pl.when(nv == R)
        def _():
            for r in range(R):
                scatter_row(blk, sl, r).start()

        @pl.when(nv < R)
        def _():
            def body(r, c):
                scatter_row(blk, sl, r).start()
                return c
            lax.fori_loop(0, nv, body, 0)

    def wait_scatter(blk, sl):
        nv = nvalid_ref[blk]

        @pl.when(nv == R)
        def _():
            pltpu.make_async_copy(ybuf.at[sl], out_hbm.at[pl.ds(0, R)], ssem.at[sl]).wait()

        @pl.when(nv < R)
        def _():
            def body(r, c):
                scatter_row(blk, sl, r).wait()
                return c
            lax.fori_loop(0, nv, body, 0)

    @pl.when(j == 0)
    def _():
        start_gather(0, 0)

    @pl.when(j + 1 < nused)
    def _():
        start_gather(j + 1, 1 - slot)

    @pl.when(j < nused)
    def _():
        wait_gather(slot)

        @pl.when(j >= 2)
        def _():
            wait_scatter(j - 2, slot)

        xb = xbuf[slot].astype(BF16)
        hgu = jnp.dot(xb, wgu_ref[0], preferred_element_type=F32) + bgu_ref[0]
        g = jnp.minimum(hgu[:, :ff], SWIGLU_LIMIT)
        u = jnp.clip(hgu[:, ff:], -SWIGLU_LIMIT, SWIGLU_LIMIT)
        hact = (u + 1.0) * (g * jax.nn.sigmoid(g * SWIGLU_ALPHA))
        y = jnp.dot(hact.astype(BF16), wdn_ref[0], preferred_element_type=F32) + bdn_ref[0]
        ybuf[slot] = y
        start_scatter(j, slot)

    @pl.when(j == nb - 1)
    def _():
        last = jnp.minimum(nused, nb) - 1
        wait_scatter(last - 1, lax.rem(last - 1, 2))
        wait_scatter(last, lax.rem(last, 2))


def _expert_blocks(x, block_e, n_used, n_valid, dst_rows, w_gu, b_gu, w_dn, b_dn):
    t, d = x.shape
    n_blocks = block_e.shape[0]
    n_exp, _, ff2 = w_gu.shape
    ff = ff2 // 2
    assert t & (t - 1) == 0, "token count must be a power of two"
    assert TOP_K * t >= 2 * MOE_BLOCK, "the drain assumes at least two computed blocks"
    out_rows = TOP_K * t
    wmap = lambda j, be, nu, nv, ds: (be[j], 0, 0)
    return pl.pallas_call(
        functools.partial(_expert_kernel, n_tokens=t, ff=ff),
        grid_spec=pltpu.PrefetchScalarGridSpec(
            num_scalar_prefetch=4,
            grid=(n_blocks,),
            in_specs=[
                pl.BlockSpec(memory_space=pl.ANY),
                pl.BlockSpec((1, d, ff2), wmap),
                pl.BlockSpec((1, 1, ff2), wmap),
                pl.BlockSpec((1, ff, d), wmap),
                pl.BlockSpec((1, 1, d), wmap),
            ],
            out_specs=pl.BlockSpec(memory_space=pl.ANY),
            scratch_shapes=[
                pltpu.VMEM((2, MOE_BLOCK, d), F32),
                pltpu.VMEM((2, MOE_BLOCK, d), F32),
                pltpu.SemaphoreType.DMA((2,)),
                pltpu.SemaphoreType.DMA((2,)),
            ],
        ),
        out_shape=jax.ShapeDtypeStruct((out_rows, d), F32),
        compiler_params=_params("arbitrary"),
        name="moe_experts",
    )(block_e, n_used, n_valid, dst_rows, x, w_gu, b_gu.reshape(n_exp, 1, ff2), w_dn, b_dn.reshape(n_exp, 1, d))


def _combine_kernel(x_ref, y0_ref, y1_ref, y2_ref, y3_ref, gate_ref, g_ref, b_ref, o_ref, obf_ref, *, alpha):
    gate = gate_ref[...]
    ffn = gate[:, 0:1] * y0_ref[...]
    for k, y_ref in enumerate((y1_ref, y2_ref, y3_ref), start=1):
        ffn = ffn + gate[:, k:k + 1] * y_ref[...]
    y = _layer_norm(alpha * x_ref[...] + ffn, g_ref[...], b_ref[...])
    o_ref[...] = y
    obf_ref[...] = y.astype(BF16)


def _combine_ln(x, ys, gate_pad, g, b, alpha, *, tm=256):
    t, d = x.shape
    tm = min(tm, t)
    nt = t // tm
    row = lambda i: (i, 0)
    fix = lambda i: (0, 0)
    y_specs = [pl.BlockSpec((tm, d), functools.partial(lambda i, k: (i + k * nt, 0), k=k)) for k in range(TOP_K)]
    return pl.pallas_call(
        functools.partial(_combine_kernel, alpha=alpha),
        grid=(nt,),
        in_specs=[pl.BlockSpec((tm, d), row)] + y_specs + [
            pl.BlockSpec((tm, LANES), row),
            pl.BlockSpec((1, d), fix),
            pl.BlockSpec((1, d), fix),
        ],
        out_specs=[pl.BlockSpec((tm, d), row), pl.BlockSpec((tm, d), row)],
        out_shape=[jax.ShapeDtypeStruct((t, d), F32), jax.ShapeDtypeStruct((t, d), BF16)],
        compiler_params=_params("parallel"),
        name="moe_combine_ln",
    )(x, ys, ys, ys, ys, gate_pad, g.reshape(1, d), b.reshape(1, d))


def _moe_layer(x, rw, rb, w_gu, b_gu, w_dn, b_dn, g, b, alpha):
    t, d = x.shape
    n_exp = rw.shape[1]
    a = t * TOP_K
    gate_pad, idx_pad = _router(x, rw, rb)
    e_flat = idx_pad[:, :TOP_K].reshape(a)
    rank, counts = _expert_ranks(e_flat, n_exp)
    padded = (counts + MOE_BLOCK - 1) // MOE_BLOCK * MOE_BLOCK
    pend = jnp.cumsum(padded)
    pstart = pend - padded
    dest = pstart[e_flat] + rank
    n_slots = a + n_exp * MOE_BLOCK
    n_blocks = n_slots // MOE_BLOCK
    block_start = jnp.arange(n_blocks, dtype=jnp.int32) * MOE_BLOCK
    block_e = jnp.minimum(jnp.searchsorted(pend, block_start, side="right"), n_exp - 1).astype(jnp.int32)
    n_used = (pend[-1] // MOE_BLOCK).astype(jnp.int32).reshape(1)
    slot_a = jnp.full((n_slots,), -1, jnp.int32).at[dest].set(
        jnp.arange(a, dtype=jnp.int32), unique_indices=True)
    valid = (slot_a >= 0).reshape(n_blocks, MOE_BLOCK)
    n_valid = jnp.sum(valid.astype(jnp.int32), axis=1)
    slot_a = slot_a.reshape(n_blocks, MOE_BLOCK)
    dst_rows = jnp.where(valid, (slot_a % TOP_K) * t + slot_a // TOP_K, 0).astype(jnp.int32)
    ys = _expert_blocks(x, block_e, n_used, n_valid, dst_rows, w_gu, b_gu, w_dn, b_dn)
    return _combine_ln(x, ys, gate_pad, g, b, alpha)


def _mlstm_layer(x, x_bf, w_in, b_gates, norm_g, w_out, g, b, alpha, *, batch, seq):
    H = MLSTM_HEADS
    n_main = w_in.shape[1] - 4 * H
    proj = _matmul(x_bf, w_in[:, :n_main].astype(BF16), jnp.zeros((n_main,), F32))
    gates_t = _mlstm_gates(x_bf, w_in[:, n_main:].T.astype(BF16), b_gates)
    h = _mlstm_scan(proj, gates_t, norm_g, batch=batch, seq=seq)
    return _matmul_res_ln(h, w_out.astype(BF16), x, g, b, alpha)


def _gmlp_layer(x, x_bf, w_in, b_in, vn_g, vn_b, w_s, b_s, w_out, g, b, alpha):
    width = w_out.shape[0]
    u = _matmul(x_bf, w_in[:, :width].astype(BF16), b_in[:width], act="gelu")
    vn = _matmul_gelu_ln(x_bf, w_in[:, width:].astype(BF16), b_in[width:], vn_g, vn_b)
    y = _gmlp_spatial(u, vn, w_s, b_s)
    return _matmul_res_ln(y, w_out.astype(BF16), x, g, b, alpha)


def kernel(x, mlstm_w_in, mlstm_b_gates, mlstm_norm_g, mlstm_w_out, gmlp_w_in, gmlp_b_in, gmlp_vnorm_g,
           gmlp_vnorm_b, gmlp_w_s, gmlp_b_s, gmlp_w_out, router_w, router_b, expert_w_gu, expert_b_gu,
           expert_w_down, expert_b_down, ln_g, ln_b):
    batch, seq, d = x.shape
    depth = ln_g.shape[0]
    alpha = float((2 * depth) ** 0.25)
    xf = x.reshape(batch * seq, d)
    x_bf = xf.astype(BF16)
    for layer in range(depth):
        j = layer // 2
        if layer % 2 == 0:
            xf, x_bf = _mlstm_layer(xf, x_bf, mlstm_w_in[j], mlstm_b_gates[j], mlstm_norm_g[j], mlstm_w_out[j],
                                    ln_g[layer, 0], ln_b[layer, 0], alpha, batch=batch, seq=seq)
        else:
            xf, x_bf = _gmlp_layer(xf, x_bf, gmlp_w_in[j], gmlp_b_in[j], gmlp_vnorm_g[j], gmlp_vnorm_b[j],
                                   gmlp_w_s[j], gmlp_b_s[j], gmlp_w_out[j],
                                   ln_g[layer, 0], ln_b[layer, 0], alpha)
        xf, x_bf = _moe_layer(xf, router_w[layer], router_b[layer],
                              expert_w_gu[layer].astype(BF16), expert_b_gu[layer],
                              expert_w_down[layer].astype(BF16), expert_b_down[layer],
                              ln_g[layer, 1], ln_b[layer, 1], alpha)
    return xf.reshape(batch, seq, d)
```

```python
import functools

import jax
import jax.numpy as jnp
from jax import lax
from jax.experimental import pallas as pl
from jax.experimental.pallas import tpu as pltpu

MLSTM_HEADS = 8
CHUNK = 128
GMLP_GROUPS = 8
TOP_K = 4
MOE_BLOCK = 128
SWIGLU_LIMIT = 7.0
SWIGLU_ALPHA = 1.702
LN_EPS = 1e-5
LANES = 128
VMEM_LIMIT = 56 * 1024 * 1024

F32 = jnp.float32
BF16 = jnp.bfloat16


def _params(*sem):
    return pltpu.CompilerParams(dimension_semantics=sem, vmem_limit_bytes=VMEM_LIMIT)


def _gelu(x):
    return 0.5 * x * (1.0 + lax.erf(x * (2.0 ** -0.5)))


def _layer_norm(z, g, b):
    mu = jnp.mean(z, axis=-1, keepdims=True)
    zc = z - mu
    var = jnp.mean(zc * zc, axis=-1, keepdims=True)
    return zc * lax.rsqrt(var + LN_EPS) * g + b


def _mm_kernel(x_ref, w_ref, b_ref, o_ref, *, act):
    acc = jnp.dot(x_ref[...], w_ref[...], preferred_element_type=F32)
    acc = acc + b_ref[...]
    if act == "gelu":
        acc = _gelu(acc)
    o_ref[...] = acc.astype(o_ref.dtype)


def _matmul(x, w, b, *, act=None, out_dtype=BF16, tm=1024, tn=1024):
    m, k = x.shape
    n = w.shape[1]
    tm, tn = min(tm, m), min(tn, n)
    assert m % tm == 0 and n % tn == 0
    return pl.pallas_call(
        functools.partial(_mm_kernel, act=act),
        grid=(m // tm, n // tn),
        in_specs=[
            pl.BlockSpec((tm, k), lambda i, j: (i, 0)),
            pl.BlockSpec((k, tn), lambda i, j: (0, j)),
            pl.BlockSpec((1, tn), lambda i, j: (0, j)),
        ],
        out_specs=pl.BlockSpec((tm, tn), lambda i, j: (i, j)),
        out_shape=jax.ShapeDtypeStruct((m, n), out_dtype),
        compiler_params=_params("parallel", "parallel"),
        name="matmul_bias_act",
    )(x, w, b.reshape(1, n).astype(F32))


def _mm_ln_kernel(*refs, alpha, residual, gelu, two_out):
    if residual:
        h_ref, w_ref, res_ref, g_ref, b_ref = refs[:5]
        outs = refs[5:]
    else:
        h_ref, w_ref, bias_ref, g_ref, b_ref = refs[:5]
        outs = refs[5:]
    acc = jnp.dot(h_ref[...], w_ref[...], preferred_element_type=F32)
    if residual:
        z = alpha * res_ref[...] + acc
    else:
        z = acc + bias_ref[...]
    if gelu:
        z = _gelu(z)
    y = _layer_norm(z, g_ref[...], b_ref[...])
    if two_out:
        outs[0][...] = y
        outs[1][...] = y.astype(BF16)
    else:
        outs[0][...] = y.astype(outs[0].dtype)


def _matmul_res_ln(h, w, res, g, b, alpha, *, tm=256):
    m, k = h.shape
    n = w.shape[1]
    tm = min(tm, m)
    assert m % tm == 0
    row = lambda i: (i, 0)
    fix = lambda i: (0, 0)
    return pl.pallas_call(
        functools.partial(_mm_ln_kernel, alpha=alpha, residual=True, gelu=False, two_out=True),
        grid=(m // tm,),
        in_specs=[
            pl.BlockSpec((tm, k), row),
            pl.BlockSpec((k, n), fix),
            pl.BlockSpec((tm, n), row),
            pl.BlockSpec((1, n), fix),
            pl.BlockSpec((1, n), fix),
        ],
        out_specs=[pl.BlockSpec((tm, n), row), pl.BlockSpec((tm, n), row)],
        out_shape=[jax.ShapeDtypeStruct((m, n), F32), jax.ShapeDtypeStruct((m, n), BF16)],
        compiler_params=_params("parallel"),
        name="matmul_residual_ln",
    )(h, w, res, g.reshape(1, n), b.reshape(1, n))


def _matmul_gelu_ln(h, w, bias, g, b, *, tm=256):
    m, k = h.shape
    n = w.shape[1]
    tm = min(tm, m)
    assert m % tm == 0
    row = lambda i: (i, 0)
    fix = lambda i: (0, 0)
    return pl.pallas_call(
        functools.partial(_mm_ln_kernel, alpha=None, residual=False, gelu=True, two_out=False),
        grid=(m // tm,),
        in_specs=[
            pl.BlockSpec((tm, k), row),
            pl.BlockSpec((k, n), fix),
            pl.BlockSpec((1, n), fix),
            pl.BlockSpec((1, n), fix),
            pl.BlockSpec((1, n), fix),
        ],
        out_specs=pl.BlockSpec((tm, n), row),
        out_shape=jax.ShapeDtypeStruct((m, n), BF16),
        compiler_params=_params("parallel"),
        name="matmul_gelu_ln",
    )(h, w, bias.reshape(1, n), g.reshape(1, n), b.reshape(1, n))


def _gates_kernel(w_ref, x_ref, b_ref, o_ref, *, heads):
    g = lax.dot_general(w_ref[...], x_ref[...], (((1,), (1,)), ((), ())),
                        preferred_element_type=F32)
    g = g + b_ref[...]
    r = lax.broadcasted_iota(jnp.int32, g.shape, 0)
    is_forget = ((r >= heads) & (r < 2 * heads)) | (r >= 3 * heads)
    log_sig = jnp.minimum(g, 0.0) - jnp.log1p(jnp.exp(-jnp.abs(g)))
    o_ref[...] = jnp.where(is_forget, log_sig, g)


def _mlstm_gates(x_bf, w_gates_t, b_gates, *, tm=1024):
    t, k = x_bf.shape
    ng = w_gates_t.shape[0]
    tm = min(tm, t)
    assert t % tm == 0
    return pl.pallas_call(
        functools.partial(_gates_kernel, heads=ng // 4),
        grid=(t // tm,),
        in_specs=[
            pl.BlockSpec((ng, k), lambda i: (0, 0)),
            pl.BlockSpec((tm, k), lambda i: (i, 0)),
            pl.BlockSpec((ng, 1), lambda i: (0, 0)),
        ],
        out_specs=pl.BlockSpec((ng, tm), lambda i: (0, i)),
        out_shape=jax.ShapeDtypeStruct((ng, t), F32),
        compiler_params=_params("parallel"),
        name="mlstm_gates",
    )(w_gates_t, x_bf, b_gates.reshape(ng, 1).astype(F32))


def _mlstm_kernel(q_ref, k_ref, v_ref, o_ref, gt_ref, ng_ref, out_ref, hfw_ref, c_ref, *, nc, scale):
    L = CHUNK
    dk = q_ref.shape[1]
    row_i = lax.broadcasted_iota(jnp.int32, (L, L), 0)
    col_i = lax.broadcasted_iota(jnp.int32, (L, L), 1)
    eye = row_i == col_i

    def chunk_step(c, carry, reverse):
        n, m = carry
        off = pl.multiple_of(c * L, L)
        qc = q_ref[pl.ds(off, L), :]
        kc = k_ref[pl.ds(off, L), :]
        vc = v_ref[pl.ds(off, L), :]
        rows = gt_ref[0, 0, c]
        r0 = 2 if reverse else 0
        i_row = rows[r0:r0 + 1, :]
        f_row = rows[r0 + 1:r0 + 2, :]
        mask = (col_i >= row_i) if reverse else (col_i <= row_i)
        b_col = jnp.sum(jnp.where(mask, f_row, 0.0), axis=1, keepdims=True)
        b_row = jnp.sum(jnp.where(eye, b_col, 0.0), axis=0, keepdims=True)
        i_col = jnp.sum(jnp.where(eye, i_row, 0.0), axis=1, keepdims=True)
        dmat = jnp.where(mask, b_col - b_row + i_row, -jnp.inf)
        inter = b_col + m
        m_t = jnp.maximum(inter, jnp.max(dmat, axis=1, keepdims=True))
        p = jnp.exp(dmat - m_t)
        sqk = lax.dot_general(qc, kc, (((1,), (1,)), ((), ())), preferred_element_type=F32)
        s = sqk * (scale * p)
        isc = jnp.exp(inter - m_t)
        qn = jnp.sum(qc.astype(F32) * n, axis=1, keepdims=True)
        den = jnp.sum(s, axis=1, keepdims=True) + isc * (scale * qn)
        c_old = c_ref[...]
        num = jnp.dot(s.astype(BF16), vc, preferred_element_type=F32)
        num = num + (isc * scale) * jnp.dot(qc, c_old.astype(BF16), preferred_element_type=F32)
        h = num / jnp.maximum(jnp.abs(den), jnp.exp(-m_t))
        b_last = jnp.sum(f_row, axis=1, keepdims=True)
        g_col = b_last - b_col + i_col
        m_new = jnp.maximum(b_last + m, jnp.max(g_col, axis=0, keepdims=True))
        decay = jnp.exp(b_last + m - m_new)
        wk = jnp.exp(g_col - m_new)
        kw = kc.astype(F32) * wk
        c_ref[...] = decay * c_old + lax.dot_general(
            kw.astype(BF16), vc, (((0,), (0,)), ((), ())), preferred_element_type=F32)
        n_new = decay * n + jnp.sum(kw, axis=0, keepdims=True)
        return h, off, (n_new, m_new)

    init = (jnp.zeros((1, dk), F32), jnp.zeros((1, 1), F32))

    def fw_body(c, carry):
        h, off, carry = chunk_step(c, carry, False)
        hfw_ref[pl.ds(off, L), :] = h
        return carry

    c_ref[...] = jnp.zeros_like(c_ref)
    lax.fori_loop(0, nc, fw_body, init)

    def bw_body(i, carry):
        h, off, carry = chunk_step(nc - 1 - i, carry, True)
        tot = hfw_ref[pl.ds(off, L), :] + h
        mu = jnp.mean(tot, axis=1, keepdims=True)
        hc = tot - mu
        var = jnp.mean(hc * hc, axis=1, keepdims=True)
        hn = hc * lax.rsqrt(var + LN_EPS) * ng_ref[...]
        og = jax.nn.sigmoid(o_ref[pl.ds(off, L), :].astype(F32))
        out_ref[pl.ds(off, L), :] = (og * hn).astype(out_ref.dtype)
        return carry

    c_ref[...] = jnp.zeros_like(c_ref)
    lax.fori_loop(0, nc, bw_body, init)


def _mlstm_scan(proj, gates_t, norm_g, *, batch, seq):
    H = MLSTM_HEADS
    L = CHUNK
    v_w = norm_g.shape[0]
    dv = v_w // H
    dk = (proj.shape[1] - 2 * v_w) // (2 * H)
    nc = seq // L
    assert seq % L == 0 and dv % LANES == 0 and dk % LANES == 0
    gt = gates_t.reshape(4, H, batch, nc, L).transpose(2, 1, 3, 0, 4)
    gt = jnp.pad(gt, ((0, 0), (0, 0), (0, 0), (0, 4), (0, 0)))
    k0 = H
    v0 = 2 * H * dk // dv
    o0 = v0 + H
    return pl.pallas_call(
        functools.partial(_mlstm_kernel, nc=nc, scale=float(dk) ** -0.5),
        grid=(batch, H),
        in_specs=[
            pl.BlockSpec((seq, dk), lambda b, h: (b, h)),
            pl.BlockSpec((seq, dk), lambda b, h: (b, k0 + h)),
            pl.BlockSpec((seq, dv), lambda b, h: (b, v0 + h)),
            pl.BlockSpec((seq, dv), lambda b, h: (b, o0 + h)),
            pl.BlockSpec((1, 1, nc, 8, L), lambda b, h: (b, h, 0, 0, 0)),
            pl.BlockSpec((1, dv), lambda b, h: (0, h)),
        ],
        out_specs=pl.BlockSpec((seq, dv), lambda b, h: (b, h)),
        out_shape=jax.ShapeDtypeStruct((batch * seq, v_w), BF16),
        scratch_shapes=[pltpu.VMEM((seq, dv), F32), pltpu.VMEM((dk, dv), F32)],
        compiler_params=_params("parallel", "parallel"),
        name="mlstm_scan",
    )(proj, proj, proj, proj, gt, norm_g.reshape(1, v_w).astype(F32))


def _spatial_kernel(ws_ref, bs_ref, v_ref, u_ref, y_ref, *, groups):
    gd = v_ref.shape[1] // groups
    for g in range(groups):
        cols = slice(g * gd, (g + 1) * gd)
        s = jnp.dot(ws_ref[g], v_ref[:, cols], preferred_element_type=F32) + bs_ref[g]
        y_ref[:, cols] = (u_ref[:, cols].astype(F32) * s).astype(y_ref.dtype)


def _gmlp_spatial(u, vn, w_s, b_s):
    t, width = u.shape
    groups, L, _ = w_s.shape
    blk = pl.BlockSpec((L, width), lambda c: (c, 0))
    return pl.pallas_call(
        functools.partial(_spatial_kernel, groups=groups),
        grid=(t // L,),
        in_specs=[
            pl.BlockSpec((groups, L, L), lambda c: (0, 0, 0)),
            pl.BlockSpec((groups, L, 1), lambda c: (0, 0, 0)),
            blk, blk,
        ],
        out_specs=blk,
        out_shape=jax.ShapeDtypeStruct((t, width), BF16),
        compiler_params=_params("parallel"),
        name="gmlp_spatial",
    )(w_s.astype(BF16), b_s.reshape(groups, L, 1).astype(F32), vn, u)


def _router_kernel(x_ref, w_ref, b_ref, gate_ref, idx_ref, *, n_experts):
    logits = jnp.dot(x_ref[...], w_ref[...], preferred_element_type=F32,
                     precision=lax.Precision.HIGHEST) + b_ref[...]
    lane = lax.broadcasted_iota(jnp.int32, logits.shape, 1)
    cur = jnp.where(lane < n_experts, logits, -jnp.inf)
    vals, idxs = [], []
    for _ in range(TOP_K):
        mx = jnp.max(cur, axis=1, keepdims=True)
        ix = jnp.min(jnp.where(cur == mx, lane, LANES), axis=1, keepdims=True)
        vals.append(mx)
        idxs.append(ix)
        cur = jnp.where(lane == ix, -jnp.inf, cur)
    exps = [jnp.exp(v - vals[0]) for v in vals]
    tot = exps[0]
    for e in exps[1:]:
        tot = tot + e
    gate = jnp.zeros(logits.shape, F32)
    idx = jnp.zeros(logits.shape, jnp.int32)
    for k in range(TOP_K):
        gate = jnp.where(lane == k, exps[k] / tot, gate)
        idx = jnp.where(lane == k, idxs[k], idx)
    gate_ref[...] = gate
    idx_ref[...] = idx


def _router(x, rw, rb, *, tm=512):
    t, d = x.shape
    e = rw.shape[1]
    assert e <= LANES
    tm = min(tm, t)
    w_pad = jnp.pad(rw.astype(F32), ((0, 0), (0, LANES - e)))
    b_pad = jnp.pad(rb.astype(F32), (0, LANES - e)).reshape(1, LANES)
    return pl.pallas_call(
        functools.partial(_router_kernel, n_experts=e),
        grid=(t // tm,),
        in_specs=[
            pl.BlockSpec((tm, d), lambda i: (i, 0)),
            pl.BlockSpec((d, LANES), lambda i: (0, 0)),
            pl.BlockSpec((1, LANES), lambda i: (0, 0)),
        ],
        out_specs=[pl.BlockSpec((tm, LANES), lambda i: (i, 0))] * 2,
        out_shape=[jax.ShapeDtypeStruct((t, LANES), F32), jax.ShapeDtypeStruct((t, LANES), jnp.int32)],
        compiler_params=_params("parallel"),
        name="moe_router",
    )(x, w_pad, b_pad)


RANK_BLOCK = 512


def _rank_kernel(e_ref, rank_ref, cnt_ref, carry_ref):
    @pl.when(pl.program_id(0) == 0)
    def _():
        carry_ref[...] = jnp.zeros_like(carry_ref)

    e_row = e_ref[0]
    ex = lax.broadcasted_iota(jnp.int32, (LANES, RANK_BLOCK), 0)
    onehot = e_row == ex
    a_i = lax.broadcasted_iota(jnp.int32, (RANK_BLOCK, RANK_BLOCK), 0)
    b_i = lax.broadcasted_iota(jnp.int32, (RANK_BLOCK, RANK_BLOCK), 1)
    before = (a_i < b_i).astype(BF16)
    within = jnp.dot(onehot.astype(BF16), before, preferred_element_type=F32)
    carry = carry_ref[...][:, :1]
    rank = jnp.sum(jnp.where(onehot, within + carry, 0.0), axis=0, keepdims=True)
    rank_ref[0] = rank.astype(jnp.int32)
    new = carry + jnp.sum(onehot.astype(F32), axis=1, keepdims=True)
    carry_ref[...] = jnp.broadcast_to(new, carry_ref.shape)
    cnt_ref[...] = jnp.broadcast_to(new, cnt_ref.shape).astype(jnp.int32)


def _expert_ranks(e_flat, n_experts):
    a = e_flat.shape[0]
    assert a % RANK_BLOCK == 0 and n_experts <= LANES
    nb = a // RANK_BLOCK
    rank, cnt = pl.pallas_call(
        _rank_kernel,
        grid=(nb,),
        in_specs=[pl.BlockSpec((1, 1, RANK_BLOCK), lambda i: (i, 0, 0))],
        out_specs=[pl.BlockSpec((1, 1, RANK_BLOCK), lambda i: (i, 0, 0)),
                   pl.BlockSpec((LANES, LANES), lambda i: (0, 0))],
        out_shape=[jax.ShapeDtypeStruct((nb, 1, RANK_BLOCK), jnp.int32),
                   jax.ShapeDtypeStruct((LANES, LANES), jnp.int32)],
        scratch_shapes=[pltpu.VMEM((LANES, LANES), F32)],
        compiler_params=_params("arbitrary"),
        name="moe_rank",
    )(e_flat.reshape(nb, 1, RANK_BLOCK))
    return rank.reshape(a), cnt[:n_experts, 0]


INVERT_CHUNK = 4096


def _invert_kernel(dest_ref, out_ref, *, n_steps, chunk, fill_chunk, n_slots):
    step = pl.program_id(0)

    @pl.when(step < n_steps)
    def _():
        base = step * fill_chunk

        def fill(i, c):
            out_ref[jnp.minimum(base + i, n_slots - 1)] = -1
            return c
        lax.fori_loop(0, fill_chunk, fill, 0, unroll=8)

    @pl.when(step >= n_steps)
    def _():
        base = (step - n_steps) * chunk

        def put(i, c):
            a = base + i
            out_ref[dest_ref[a]] = a
            return c
        lax.fori_loop(0, chunk, put, 0, unroll=8)


def _invert_dest(dest, n_slots):
    a = dest.shape[0]
    chunk = min(INVERT_CHUNK, a)
    assert a % chunk == 0
    n_steps = a // chunk
    fill_chunk = -(-n_slots // n_steps)
    smem = pl.BlockSpec(memory_space=pltpu.SMEM)
    return pl.pallas_call(
        functools.partial(_invert_kernel, n_steps=n_steps, chunk=chunk, fill_chunk=fill_chunk, n_slots=n_slots),
        grid=(2 * n_steps,),
        in_specs=[smem],
        out_specs=smem,
        out_shape=jax.ShapeDtypeStruct((n_slots,), jnp.int32),
        compiler_params=_params("arbitrary"),
        name="moe_invert",
    )(dest)


def _expert_kernel(be_ref, dst_ref, x_hbm, wgu_ref, bgu_ref, wdn_ref, bdn_ref, out_hbm,
                   xbuf0, xbuf1, ybuf0, ybuf1, hbuf, gsem, ssem, *, n_tokens, n_blocks, ff):
    j = pl.program_id(0)
    R = MOE_BLOCK
    xbufs = (xbuf0, xbuf1)
    ybufs = (ybuf0, ybuf1)

    def start_gather(blk, s):
        for r in range(R):
            tok = dst_ref[blk, r] & (n_tokens - 1)
            pltpu.make_async_copy(x_hbm.at[pl.ds(tok, 1)], xbufs[s].at[pl.ds(r, 1)], gsem.at[s]).start()

    def wait_gather(s):
        pltpu.make_async_copy(x_hbm.at[pl.ds(0, R)], xbufs[s], gsem.at[s]).wait()

    def start_scatter(blk, s):
        for r in range(R):
            pltpu.make_async_copy(ybufs[s].at[pl.ds(r, 1)], out_hbm.at[pl.ds(dst_ref[blk, r], 1)],
                                  ssem.at[s]).start()

    def wait_scatter(s):
        pltpu.make_async_copy(ybufs[s], out_hbm.at[pl.ds(0, R)], ssem.at[s]).wait()

    def up_proj(s):
        xb = xbufs[s][...].astype(BF16)
        hgu = jnp.dot(xb, wgu_ref[0], preferred_element_type=F32) + bgu_ref[0]
        g = jnp.minimum(hgu[:, :ff], SWIGLU_LIMIT)
        u = jnp.clip(hgu[:, ff:], -SWIGLU_LIMIT, SWIGLU_LIMIT)
        hbuf[...] = ((u + 1.0) * (g * jax.nn.sigmoid(g * SWIGLU_ALPHA))).astype(BF16)

    def down_proj(s):
        ybufs[s][...] = jnp.dot(hbuf[...], wdn_ref[0], preferred_element_type=F32) + bdn_ref[0]

    @pl.when(j == 0)
    def _():
        start_gather(0, 0)
        wait_gather(0)
        start_gather(1, 1)
        up_proj(0)
        down_proj(0)

    for s in (0, 1):
        @pl.when((j > 0) & (lax.rem(j, 2) == s))
        def _(s=s):
            wait_gather(s)
            start_gather(jnp.minimum(j + 1, n_blocks - 1), 1 - s)
            up_proj(s)

            @pl.when(j >= 2)
            def _():
                wait_scatter(s)

            start_scatter(j - 1, 1 - s)
            down_proj(s)

    @pl.when(j == n_blocks - 1)
    def _():
        s_last = (n_blocks - 1) % 2
        start_scatter(n_blocks - 1, s_last)
        wait_gather(1 - s_last)
        wait_scatter(1 - s_last)
        wait_scatter(s_last)


def _expert_blocks(x, block_e, dst_rows, w_gu, b_gu, w_dn, b_dn):
    t, d = x.shape
    n_blocks = block_e.shape[0]
    n_exp, _, ff2 = w_gu.shape
    ff = ff2 // 2
    assert t & (t - 1) == 0, "token count must be a power of two"
    assert n_blocks >= 2
    wmap = lambda j, be, ds: (be[j], 0, 0)
    return pl.pallas_call(
        functools.partial(_expert_kernel, n_tokens=t, n_blocks=n_blocks, ff=ff),
        grid_spec=pltpu.PrefetchScalarGridSpec(
            num_scalar_prefetch=2,
            grid=(n_blocks,),
            in_specs=[
                pl.BlockSpec(memory_space=pl.ANY),
                pl.BlockSpec((1, d, ff2), wmap),
                pl.BlockSpec((1, 1, ff2), wmap),
                pl.BlockSpec((1, ff, d), wmap),
                pl.BlockSpec((1, 1, d), wmap),
            ],
            out_specs=pl.BlockSpec(memory_space=pl.ANY),
            scratch_shapes=[
                pltpu.VMEM((MOE_BLOCK, d), F32),
                pltpu.VMEM((MOE_BLOCK, d), F32),
                pltpu.VMEM((MOE_BLOCK, d), F32),
                pltpu.VMEM((MOE_BLOCK, d), F32),
                pltpu.VMEM((MOE_BLOCK, ff), BF16),
                pltpu.SemaphoreType.DMA((2,)),
                pltpu.SemaphoreType.DMA((2,)),
            ],
        ),
        out_shape=jax.ShapeDtypeStruct((n_blocks * MOE_BLOCK, d), F32),
        compiler_params=_params("arbitrary"),
        name="moe_experts",
    )(block_e, dst_rows, x, w_gu, b_gu.reshape(n_exp, 1, ff2), w_dn, b_dn.reshape(n_exp, 1, d))


def _combine_kernel(x_ref, y0_ref, y1_ref, y2_ref, y3_ref, gate_ref, g_ref, b_ref, o_ref, obf_ref, *, alpha):
    gate = gate_ref[...]
    ffn = gate[:, 0:1] * y0_ref[...]
    for k, y_ref in enumerate((y1_ref, y2_ref, y3_ref), start=1):
        ffn = ffn + gate[:, k:k + 1] * y_ref[...]
    y = _layer_norm(alpha * x_ref[...] + ffn, g_ref[...], b_ref[...])
    o_ref[...] = y
    obf_ref[...] = y.astype(BF16)


def _combine_ln(x, ys, gate_pad, g, b, alpha, *, tm=256):
    t, d = x.shape
    tm = min(tm, t)
    nt = t // tm
    row = lambda i: (i, 0)
    fix = lambda i: (0, 0)
    y_specs = [pl.BlockSpec((tm, d), functools.partial(lambda i, k: (i + k * nt, 0), k=k)) for k in range(TOP_K)]
    return pl.pallas_call(
        functools.partial(_combine_kernel, alpha=alpha),
        grid=(nt,),
        in_specs=[pl.BlockSpec((tm, d), row)] + y_specs + [
            pl.BlockSpec((tm, LANES), row),
            pl.BlockSpec((1, d), fix),
            pl.BlockSpec((1, d), fix),
        ],
        out_specs=[pl.BlockSpec((tm, d), row), pl.BlockSpec((tm, d), row)],
        out_shape=[jax.ShapeDtypeStruct((t, d), F32), jax.ShapeDtypeStruct((t, d), BF16)],
        compiler_params=_params("parallel"),
        name="moe_combine_ln",
    )(x, ys, ys, ys, ys, gate_pad, g.reshape(1, d), b.reshape(1, d))


def _moe_layer(x, rw, rb, w_gu, b_gu, w_dn, b_dn, g, b, alpha):
    t, d = x.shape
    n_exp = rw.shape[1]
    a = t * TOP_K
    gate_pad, idx_pad = _router(x, rw, rb)
    e_flat = idx_pad[:, :TOP_K].reshape(a)
    rank, counts = _expert_ranks(e_flat, n_exp)
    padded = (counts + MOE_BLOCK - 1) // MOE_BLOCK * MOE_BLOCK
    pend = jnp.cumsum(padded)
    pstart = pend - padded
    dest = pstart[e_flat] + rank
    n_slots = a + n_exp * MOE_BLOCK
    n_blocks = n_slots // MOE_BLOCK
    block_start = jnp.arange(n_blocks, dtype=jnp.int32) * MOE_BLOCK
    block_e = jnp.minimum(jnp.sum((block_start[:, None] >= pend[None, :]).astype(jnp.int32), axis=1), n_exp - 1)
    slot_a = _invert_dest(dest, n_slots)
    valid = slot_a >= 0
    pad_idx = jnp.cumsum(jnp.logical_not(valid).astype(jnp.int32)) - 1
    dst_rows = jnp.where(valid, (slot_a % TOP_K) * t + slot_a // TOP_K, a + pad_idx).astype(jnp.int32)
    ys = _expert_blocks(x, block_e, dst_rows.reshape(n_blocks, MOE_BLOCK), w_gu, b_gu, w_dn, b_dn)
    return _combine_ln(x, ys, gate_pad, g, b, alpha)


def _mlstm_layer(x, x_bf, w_in, b_gates, norm_g, w_out, g, b, alpha, *, batch, seq):
    H = MLSTM_HEADS
    n_main = w_in.shape[1] - 4 * H
    proj = _matmul(x_bf, w_in[:, :n_main].astype(BF16), jnp.zeros((n_main,), F32))
    gates_t = _mlstm_gates(x_bf, w_in[:, n_main:].T.astype(BF16), b_gates)
    h = _mlstm_scan(proj, gates_t, norm_g, batch=batch, seq=seq)
    return _matmul_res_ln(h, w_out.astype(BF16), x, g, b, alpha)


def _gmlp_layer(x, x_bf, w_in, b_in, vn_g, vn_b, w_s, b_s, w_out, g, b, alpha):
    width = w_out.shape[0]
    u = _matmul(x_bf, w_in[:, :width].astype(BF16), b_in[:width], act="gelu")
    vn = _matmul_gelu_ln(x_bf, w_in[:, width:].astype(BF16), b_in[width:], vn_g, vn_b)
    y = _gmlp_spatial(u, vn, w_s, b_s)
    return _matmul_res_ln(y, w_out.astype(BF16), x, g, b, alpha)


def kernel(x, mlstm_w_in, mlstm_b_gates, mlstm_norm_g, mlstm_w_out, gmlp_w_in, gmlp_b_in, gmlp_vnorm_g,
           gmlp_vnorm_b, gmlp_w_s, gmlp_b_s, gmlp_w_out, router_w, router_b, expert_w_gu, expert_b_gu,
           expert_w_down, expert_b_down, ln_g, ln_b):
    batch, seq, d = x.shape
    depth = ln_g.shape[0]
    alpha = float((2 * depth) ** 0.25)
    xf = x.reshape(batch * seq, d)
    x_bf = xf.astype(BF16)
    for layer in range(depth):
        j = layer // 2
        if layer % 2 == 0:
            xf, x_bf = _mlstm_layer(xf, x_bf, mlstm_w_in[j], mlstm_b_gates[j], mlstm_norm_g[j], mlstm_w_out[j],
                                    ln_g[layer, 0], ln_b[layer, 0], alpha, batch=batch, seq=seq)
        else:
            xf, x_bf = _gmlp_layer(xf, x_bf, gmlp_w_in[j], gmlp_b_in[j], gmlp_vnorm_g[j], gmlp_vnorm_b[j],
                                   gmlp_w_s[j], gmlp_b_s[j], gmlp_w_out[j],
                                   ln_g[layer, 0], ln_b[layer, 0], alpha)
        xf, x_bf = _moe_layer(xf, router_w[layer], router_b[layer],
                              expert_w_gu[layer].astype(BF16), expert_b_gu[layer],
                              expert_w_down[layer].astype(BF16), expert_b_down[layer],
                              ln_g[layer, 1], ln_b[layer, 1], alpha)
    return xf.reshape(batch, seq, d)
```

```python
import functools

import jax
import jax.numpy as jnp
from jax import lax
from jax.experimental import pallas as pl
from jax.experimental.pallas import tpu as pltpu

MLSTM_HEADS = 8
CHUNK = 128
GMLP_GROUPS = 8
TOP_K = 4
MOE_BLOCK = 128
SWIGLU_LIMIT = 7.0
SWIGLU_ALPHA = 1.702
LN_EPS = 1e-5
LANES = 128
VMEM_LIMIT = 56 * 1024 * 1024

F32 = jnp.float32
BF16 = jnp.bfloat16


def _params(*sem):
    return pltpu.CompilerParams(dimension_semantics=sem, vmem_limit_bytes=VMEM_LIMIT)


def _gelu(x):
    return 0.5 * x * (1.0 + lax.erf(x * (2.0 ** -0.5)))


def _layer_norm(z, g, b):
    mu = jnp.mean(z, axis=-1, keepdims=True)
    zc = z - mu
    var = jnp.mean(zc * zc, axis=-1, keepdims=True)
    return zc * lax.rsqrt(var + LN_EPS) * g + b


def _mm_kernel(x_ref, w_ref, b_ref, o_ref, wbf_ref, *, act):
    @pl.when(pl.program_id(1) == 0)
    def _():
        wbf_ref[...] = w_ref[...].astype(BF16)

    acc = jnp.dot(x_ref[...], wbf_ref[...], preferred_element_type=F32)
    acc = acc + b_ref[...]
    if act == "gelu":
        acc = _gelu(acc)
    o_ref[...] = acc.astype(o_ref.dtype)


def _matmul(x, w_all, layer, col0, n, b, *, act=None, out_dtype=BF16, tm=1024, tn=1024):
    m, k = x.shape
    tm, tn = min(tm, m), min(tn, n)
    assert m % tm == 0 and n % tn == 0 and col0 % tn == 0
    c0 = col0 // tn
    return pl.pallas_call(
        functools.partial(_mm_kernel, act=act),
        grid=(n // tn, m // tm),
        in_specs=[
            pl.BlockSpec((tm, k), lambda j, i: (i, 0)),
            pl.BlockSpec((None, k, tn), lambda j, i: (layer, 0, c0 + j)),
            pl.BlockSpec((1, tn), lambda j, i: (0, j)),
        ],
        out_specs=pl.BlockSpec((tm, tn), lambda j, i: (i, j)),
        out_shape=jax.ShapeDtypeStruct((m, n), out_dtype),
        scratch_shapes=[pltpu.VMEM((k, tn), BF16)],
        compiler_params=_params("parallel", "arbitrary"),
        name="matmul_bias_act",
    )(x, w_all, b.reshape(1, n).astype(F32))


def _mm_ln_kernel(*refs, alpha, residual, gelu, two_out):
    wbf_ref = refs[-1]
    refs = refs[:-1]
    if residual:
        h_ref, w_ref, res_ref, g_ref, b_ref = refs[:5]
        outs = refs[5:]
    else:
        h_ref, w_ref, bias_ref, g_ref, b_ref = refs[:5]
        outs = refs[5:]

    @pl.when(pl.program_id(0) == 0)
    def _():
        wbf_ref[...] = w_ref[...].astype(BF16)

    acc = jnp.dot(h_ref[...], wbf_ref[...], preferred_element_type=F32)
    if residual:
        z = alpha * res_ref[...] + acc
    else:
        z = acc + bias_ref[...]
    if gelu:
        z = _gelu(z)
    y = _layer_norm(z, g_ref[...], b_ref[...])
    if two_out:
        outs[0][...] = y
        outs[1][...] = y.astype(BF16)
    else:
        outs[0][...] = y.astype(outs[0].dtype)


def _weight_spec(k, n, layer, colblk):
    return pl.BlockSpec((None, k, n), lambda i: (layer, 0, colblk), pipeline_mode=pl.Buffered(1))


def _matmul_res_ln(h, w_all, layer, res, g, b, alpha, *, tm=256):
    m, k = h.shape
    n = w_all.shape[2]
    tm = min(tm, m)
    assert m % tm == 0
    row = lambda i: (i, 0)
    fix = lambda i: (0, 0)
    return pl.pallas_call(
        functools.partial(_mm_ln_kernel, alpha=alpha, residual=True, gelu=False, two_out=True),
        grid=(m // tm,),
        in_specs=[
            pl.BlockSpec((tm, k), row),
            _weight_spec(k, n, layer, 0),
            pl.BlockSpec((tm, n), row),
            pl.BlockSpec((1, n), fix),
            pl.BlockSpec((1, n), fix),
        ],
        out_specs=[pl.BlockSpec((tm, n), row), pl.BlockSpec((tm, n), row)],
        out_shape=[jax.ShapeDtypeStruct((m, n), F32), jax.ShapeDtypeStruct((m, n), BF16)],
        scratch_shapes=[pltpu.VMEM((k, n), BF16)],
        compiler_params=_params("arbitrary"),
        name="matmul_residual_ln",
    )(h, w_all, res, g.reshape(1, n), b.reshape(1, n))


def _matmul_gelu_ln(h, w_all, layer, colblk, n, bias, g, b, *, tm=256):
    m, k = h.shape
    tm = min(tm, m)
    assert m % tm == 0
    row = lambda i: (i, 0)
    fix = lambda i: (0, 0)
    return pl.pallas_call(
        functools.partial(_mm_ln_kernel, alpha=None, residual=False, gelu=True, two_out=False),
        grid=(m // tm,),
        in_specs=[
            pl.BlockSpec((tm, k), row),
            _weight_spec(k, n, layer, colblk),
            pl.BlockSpec((1, n), fix),
            pl.BlockSpec((1, n), fix),
            pl.BlockSpec((1, n), fix),
        ],
        out_specs=pl.BlockSpec((tm, n), row),
        out_shape=jax.ShapeDtypeStruct((m, n), BF16),
        scratch_shapes=[pltpu.VMEM((k, n), BF16)],
        compiler_params=_params("arbitrary"),
        name="matmul_gelu_ln",
    )(h, w_all, bias.reshape(1, n), g.reshape(1, n), b.reshape(1, n))


def _gates_kernel(w_ref, x_ref, b_ref, o_ref, *, heads):
    g = lax.dot_general(w_ref[...], x_ref[...], (((1,), (1,)), ((), ())),
                        preferred_element_type=F32)
    g = g + b_ref[...]
    r = lax.broadcasted_iota(jnp.int32, g.shape, 0)
    is_forget = ((r >= heads) & (r < 2 * heads)) | (r >= 3 * heads)
    log_sig = jnp.minimum(g, 0.0) - jnp.log1p(jnp.exp(-jnp.abs(g)))
    o_ref[...] = jnp.where(is_forget, log_sig, g)


def _mlstm_gates(x_bf, w_gates_t, b_gates, *, tm=1024):
    t, k = x_bf.shape
    ng = w_gates_t.shape[0]
    tm = min(tm, t)
    assert t % tm == 0
    return pl.pallas_call(
        functools.partial(_gates_kernel, heads=ng // 4),
        grid=(t // tm,),
        in_specs=[
            pl.BlockSpec((ng, k), lambda i: (0, 0)),
            pl.BlockSpec((tm, k), lambda i: (i, 0)),
            pl.BlockSpec((ng, 1), lambda i: (0, 0)),
        ],
        out_specs=pl.BlockSpec((ng, tm), lambda i: (0, i)),
        out_shape=jax.ShapeDtypeStruct((ng, t), F32),
        compiler_params=_params("parallel"),
        name="mlstm_gates",
    )(w_gates_t, x_bf, b_gates.reshape(ng, 1).astype(F32))


def _mlstm_kernel(q_ref, k_ref, v_ref, o_ref, gt_ref, ng_ref, out_ref, hfw_ref, c_ref, *, nc, scale):
    L = CHUNK
    dk = q_ref.shape[1]
    row_i = lax.broadcasted_iota(jnp.int32, (L, L), 0)
    col_i = lax.broadcasted_iota(jnp.int32, (L, L), 1)
    eye = row_i == col_i

    def chunk_step(c, carry, reverse):
        n, m = carry
        off = pl.multiple_of(c * L, L)
        qc = q_ref[pl.ds(off, L), :]
        kc = k_ref[pl.ds(off, L), :]
        vc = v_ref[pl.ds(off, L), :]
        rows = gt_ref[0, 0, c]
        r0 = 2 if reverse else 0
        i_row = rows[r0:r0 + 1, :]
        f_row = rows[r0 + 1:r0 + 2, :]
        mask = (col_i >= row_i) if reverse else (col_i <= row_i)
        b_col = jnp.sum(jnp.where(mask, f_row, 0.0), axis=1, keepdims=True)
        b_row = jnp.sum(jnp.where(eye, b_col, 0.0), axis=0, keepdims=True)
        i_col = jnp.sum(jnp.where(eye, i_row, 0.0), axis=1, keepdims=True)
        dmat = jnp.where(mask, b_col - b_row + i_row, -jnp.inf)
        inter = b_col + m
        m_t = jnp.maximum(inter, jnp.max(dmat, axis=1, keepdims=True))
        p = jnp.exp(dmat - m_t)
        sqk = lax.dot_general(qc, kc, (((1,), (1,)), ((), ())), preferred_element_type=F32)
        s = sqk * (scale * p)
        isc = jnp.exp(inter - m_t)
        qn = jnp.sum(qc.astype(F32) * n, axis=1, keepdims=True)
        den = jnp.sum(s, axis=1, keepdims=True) + isc * (scale * qn)
        c_old = c_ref[...]
        num = jnp.dot(s.astype(BF16), vc, preferred_element_type=F32)
        num = num + (isc * scale) * jnp.dot(qc, c_old.astype(BF16), preferred_element_type=F32)
        h = num / jnp.maximum(jnp.abs(den), jnp.exp(-m_t))
        b_last = jnp.sum(f_row, axis=1, keepdims=True)
        g_col = b_last - b_col + i_col
        m_new = jnp.maximum(b_last + m, jnp.max(g_col, axis=0, keepdims=True))
        decay = jnp.exp(b_last + m - m_new)
        wk = jnp.exp(g_col - m_new)
        kw = kc.astype(F32) * wk
        c_ref[...] = decay * c_old + lax.dot_general(
            kw.astype(BF16), vc, (((0,), (0,)), ((), ())), preferred_element_type=F32)
        n_new = decay * n + jnp.sum(kw, axis=0, keepdims=True)
        return h, off, (n_new, m_new)

    init = (jnp.zeros((1, dk), F32), jnp.zeros((1, 1), F32))

    def fw_body(c, carry):
        h, off, carry = chunk_step(c, carry, False)
        hfw_ref[pl.ds(off, L), :] = h
        return carry

    c_ref[...] = jnp.zeros_like(c_ref)
    lax.fori_loop(0, nc, fw_body, init)

    def bw_body(i, carry):
        h, off, carry = chunk_step(nc - 1 - i, carry, True)
        tot = hfw_ref[pl.ds(off, L), :] + h
        mu = jnp.mean(tot, axis=1, keepdims=True)
        hc = tot - mu
        var = jnp.mean(hc * hc, axis=1, keepdims=True)
        hn = hc * lax.rsqrt(var + LN_EPS) * ng_ref[...]
        og = jax.nn.sigmoid(o_ref[pl.ds(off, L), :].astype(F32))
        out_ref[pl.ds(off, L), :] = (og * hn).astype(out_ref.dtype)
        return carry

    c_ref[...] = jnp.zeros_like(c_ref)
    lax.fori_loop(0, nc, bw_body, init)


def _mlstm_scan(proj, gates_t, norm_g, *, batch, seq):
    H = MLSTM_HEADS
    L = CHUNK
    v_w = norm_g.shape[0]
    dv = v_w // H
    dk = (proj.shape[1] - 2 * v_w) // (2 * H)
    nc = seq // L
    assert seq % L == 0 and dv % LANES == 0 and dk % LANES == 0
    gt = gates_t.reshape(4, H, batch, nc, L).transpose(2, 1, 3, 0, 4)
    gt = jnp.pad(gt, ((0, 0), (0, 0), (0, 0), (0, 4), (0, 0)))
    k0 = H
    v0 = 2 * H * dk // dv
    o0 = v0 + H
    return pl.pallas_call(
        functools.partial(_mlstm_kernel, nc=nc, scale=float(dk) ** -0.5),
        grid=(batch, H),
        in_specs=[
            pl.BlockSpec((seq, dk), lambda b, h: (b, h)),
            pl.BlockSpec((seq, dk), lambda b, h: (b, k0 + h)),
            pl.BlockSpec((seq, dv), lambda b, h: (b, v0 + h)),
            pl.BlockSpec((seq, dv), lambda b, h: (b, o0 + h)),
            pl.BlockSpec((1, 1, nc, 8, L), lambda b, h: (b, h, 0, 0, 0)),
            pl.BlockSpec((1, dv), lambda b, h: (0, h)),
        ],
        out_specs=pl.BlockSpec((seq, dv), lambda b, h: (b, h)),
        out_shape=jax.ShapeDtypeStruct((batch * seq, v_w), BF16),
        scratch_shapes=[pltpu.VMEM((seq, dv), F32), pltpu.VMEM((dk, dv), F32)],
        compiler_params=_params("parallel", "parallel"),
        name="mlstm_scan",
    )(proj, proj, proj, proj, gt, norm_g.reshape(1, v_w).astype(F32))


def _spatial_kernel(ws_ref, bs_ref, v_ref, u_ref, y_ref, *, groups):
    gd = v_ref.shape[1] // groups
    for g in range(groups):
        cols = slice(g * gd, (g + 1) * gd)
        s = jnp.dot(ws_ref[g], v_ref[:, cols], preferred_element_type=F32) + bs_ref[g]
        y_ref[:, cols] = (u_ref[:, cols].astype(F32) * s).astype(y_ref.dtype)


def _gmlp_spatial(u, vn, w_s, b_s):
    t, width = u.shape
    groups, L, _ = w_s.shape
    blk = pl.BlockSpec((L, width), lambda c: (c, 0))
    return pl.pallas_call(
        functools.partial(_spatial_kernel, groups=groups),
        grid=(t // L,),
        in_specs=[
            pl.BlockSpec((groups, L, L), lambda c: (0, 0, 0)),
            pl.BlockSpec((groups, L, 1), lambda c: (0, 0, 0)),
            blk, blk,
        ],
        out_specs=blk,
        out_shape=jax.ShapeDtypeStruct((t, width), BF16),
        compiler_params=_params("parallel"),
        name="gmlp_spatial",
    )(w_s.astype(BF16), b_s.reshape(groups, L, 1).astype(F32), vn, u)


def _router_kernel(x_ref, w_ref, b_ref, gate_ref, idx_ref, *, n_experts):
    logits = jnp.dot(x_ref[...], w_ref[...], preferred_element_type=F32,
                     precision=lax.Precision.HIGHEST) + b_ref[...]
    lane = lax.broadcasted_iota(jnp.int32, logits.shape, 1)
    cur = jnp.where(lane < n_experts, logits, -jnp.inf)
    vals, idxs = [], []
    for _ in range(TOP_K):
        mx = jnp.max(cur, axis=1, keepdims=True)
        ix = jnp.min(jnp.where(cur == mx, lane, LANES), axis=1, keepdims=True)
        vals.append(mx)
        idxs.append(ix)
        cur = jnp.where(lane == ix, -jnp.inf, cur)
    exps = [jnp.exp(v - vals[0]) for v in vals]
    tot = exps[0]
    for e in exps[1:]:
        tot = tot + e
    gate = jnp.zeros(logits.shape, F32)
    idx = jnp.zeros(logits.shape, jnp.int32)
    for k in range(TOP_K):
        gate = jnp.where(lane == k, exps[k] / tot, gate)
        idx = jnp.where(lane == k, idxs[k], idx)
    gate_ref[...] = gate
    idx_ref[...] = idx


def _router(x, rw, rb, *, tm=512):
    t, d = x.shape
    e = rw.shape[1]
    assert e <= LANES
    tm = min(tm, t)
    w_pad = jnp.pad(rw.astype(F32), ((0, 0), (0, LANES - e)))
    b_pad = jnp.pad(rb.astype(F32), (0, LANES - e)).reshape(1, LANES)
    return pl.pallas_call(
        functools.partial(_router_kernel, n_experts=e),
        grid=(t // tm,),
        in_specs=[
            pl.BlockSpec((tm, d), lambda i: (i, 0)),
            pl.BlockSpec((d, LANES), lambda i: (0, 0)),
            pl.BlockSpec((1, LANES), lambda i: (0, 0)),
        ],
        out_specs=[pl.BlockSpec((tm, LANES), lambda i: (i, 0))] * 2,
        out_shape=[jax.ShapeDtypeStruct((t, LANES), F32), jax.ShapeDtypeStruct((t, LANES), jnp.int32)],
        compiler_params=_params("parallel"),
        name="moe_router",
    )(x, w_pad, b_pad)


RANK_BLOCK = 512


def _rank_kernel(e_ref, rank_ref, cnt_ref, carry_ref):
    @pl.when(pl.program_id(0) == 0)
    def _():
        carry_ref[...] = jnp.zeros_like(carry_ref)

    e_row = e_ref[0]
    ex = lax.broadcasted_iota(jnp.int32, (LANES, RANK_BLOCK), 0)
    onehot = e_row == ex
    a_i = lax.broadcasted_iota(jnp.int32, (RANK_BLOCK, RANK_BLOCK), 0)
    b_i = lax.broadcasted_iota(jnp.int32, (RANK_BLOCK, RANK_BLOCK), 1)
    before = (a_i < b_i).astype(BF16)
    within = jnp.dot(onehot.astype(BF16), before, preferred_element_type=F32)
    carry = carry_ref[...][:, :1]
    rank = jnp.sum(jnp.where(onehot, within + carry, 0.0), axis=0, keepdims=True)
    rank_ref[0] = rank.astype(jnp.int32)
    new = carry + jnp.sum(onehot.astype(F32), axis=1, keepdims=True)
    carry_ref[...] = jnp.broadcast_to(new, carry_ref.shape)
    cnt_ref[...] = jnp.broadcast_to(new, cnt_ref.shape).astype(jnp.int32)


def _expert_ranks(e_flat, n_experts):
    a = e_flat.shape[0]
    assert a % RANK_BLOCK == 0 and n_experts <= LANES
    nb = a // RANK_BLOCK
    rank, cnt = pl.pallas_call(
        _rank_kernel,
        grid=(nb,),
        in_specs=[pl.BlockSpec((1, 1, RANK_BLOCK), lambda i: (i, 0, 0))],
        out_specs=[pl.BlockSpec((1, 1, RANK_BLOCK), lambda i: (i, 0, 0)),
                   pl.BlockSpec((LANES, LANES), lambda i: (0, 0))],
        out_shape=[jax.ShapeDtypeStruct((nb, 1, RANK_BLOCK), jnp.int32),
                   jax.ShapeDtypeStruct((LANES, LANES), jnp.int32)],
        scratch_shapes=[pltpu.VMEM((LANES, LANES), F32)],
        compiler_params=_params("arbitrary"),
        name="moe_rank",
    )(e_flat.reshape(nb, 1, RANK_BLOCK))
    return rank.reshape(a), cnt[:n_experts, 0]


INVERT_CHUNK = 4096


def _invert_kernel(dest_ref, lo_ref, hi_ref, out_ref, *, chunk, n_exp, n_slots):
    step = pl.program_id(0)

    def fill_range(lo, hi):
        def fill(s, c):
            out_ref[s] = -1
            return c
        lax.fori_loop(lo, hi, fill, 0)

    @pl.when(step == 0)
    def _():
        for e in range(n_exp):
            fill_range(lo_ref[e], hi_ref[e])
        fill_range(hi_ref[n_exp - 1], n_slots)

    @pl.when(step > 0)
    def _():
        base = (step - 1) * chunk

        def put(i, c):
            a = base + i
            out_ref[dest_ref[a]] = a
            return c
        lax.fori_loop(0, chunk, put, 0, unroll=8)


def _invert_dest(dest, pad_lo, pad_hi, n_slots):
    a = dest.shape[0]
    chunk = min(INVERT_CHUNK, a)
    assert a % chunk == 0
    smem = pl.BlockSpec(memory_space=pltpu.SMEM)
    return pl.pallas_call(
        functools.partial(_invert_kernel, chunk=chunk, n_exp=pad_lo.shape[0], n_slots=n_slots),
        grid=(a // chunk + 1,),
        in_specs=[smem, smem, smem],
        out_specs=smem,
        out_shape=jax.ShapeDtypeStruct((n_slots,), jnp.int32),
        compiler_params=_params("arbitrary"),
        name="moe_invert",
    )(dest, pad_lo, pad_hi)


def _expert_kernel(be_ref, dst_ref, x_hbm, wgu_ref, bgu_ref, wdn_ref, bdn_ref, out_hbm,
                   xbuf0, xbuf1, ybuf0, ybuf1, hbuf, wgu_bf, wdn_bf, gsem, ssem, *, n_tokens, n_blocks, ff):
    j = pl.program_id(0)
    R = MOE_BLOCK
    xbufs = (xbuf0, xbuf1)
    ybufs = (ybuf0, ybuf1)

    @pl.when((j == 0) | (be_ref[j] != be_ref[jnp.maximum(j - 1, 0)]))
    def _():
        wgu_bf[...] = wgu_ref[...].astype(BF16)
        wdn_bf[...] = wdn_ref[...].astype(BF16)

    def start_gather(blk, s):
        for r in range(R):
            tok = dst_ref[blk, r] & (n_tokens - 1)
            pltpu.make_async_copy(x_hbm.at[pl.ds(tok, 1)], xbufs[s].at[pl.ds(r, 1)], gsem.at[s]).start()

    def wait_gather(s):
        pltpu.make_async_copy(x_hbm.at[pl.ds(0, R)], xbufs[s], gsem.at[s]).wait()

    def start_scatter(blk, s):
        for r in range(R):
            pltpu.make_async_copy(ybufs[s].at[pl.ds(r, 1)], out_hbm.at[pl.ds(dst_ref[blk, r], 1)],
                                  ssem.at[s]).start()

    def wait_scatter(s):
        pltpu.make_async_copy(ybufs[s], out_hbm.at[pl.ds(0, R)], ssem.at[s]).wait()

    def up_proj(s):
        xb = xbufs[s][...].astype(BF16)
        hgu = jnp.dot(xb, wgu_bf[...], preferred_element_type=F32) + bgu_ref[0]
        g = jnp.minimum(hgu[:, :ff], SWIGLU_LIMIT)
        u = jnp.clip(hgu[:, ff:], -SWIGLU_LIMIT, SWIGLU_LIMIT)
        hbuf[...] = ((u + 1.0) * (g * jax.nn.sigmoid(g * SWIGLU_ALPHA))).astype(BF16)

    def down_proj(s):
        ybufs[s][...] = jnp.dot(hbuf[...], wdn_bf[...], preferred_element_type=F32) + bdn_ref[0]

    @pl.when(j == 0)
    def _():
        start_gather(0, 0)
        wait_gather(0)
        start_gather(1, 1)
        up_proj(0)
        down_proj(0)

    for s in (0, 1):
        @pl.when((j > 0) & (lax.rem(j, 2) == s))
        def _(s=s):
            wait_gather(s)
            start_gather(jnp.minimum(j + 1, n_blocks - 1), 1 - s)
            up_proj(s)

            @pl.when(j >= 2)
            def _():
                wait_scatter(s)

            start_scatter(j - 1, 1 - s)
            down_proj(s)

    @pl.when(j == n_blocks - 1)
    def _():
        s_last = (n_blocks - 1) % 2
        start_scatter(n_blocks - 1, s_last)
        wait_gather(1 - s_last)
        wait_scatter(1 - s_last)
        wait_scatter(s_last)


def _expert_blocks(x, block_e, dst_rows, w_gu_all, w_dn_all, layer, b_gu, b_dn):
    t, d = x.shape
    n_blocks = block_e.shape[0]
    _, n_exp, _, ff2 = w_gu_all.shape
    ff = ff2 // 2
    assert t & (t - 1) == 0, "token count must be a power of two"
    assert n_blocks >= 2
    wmap = lambda j, be, ds: (layer, be[j], 0, 0)
    bmap = lambda j, be, ds: (be[j], 0, 0)
    return pl.pallas_call(
        functools.partial(_expert_kernel, n_tokens=t, n_blocks=n_blocks, ff=ff),
        grid_spec=pltpu.PrefetchScalarGridSpec(
            num_scalar_prefetch=2,
            grid=(n_blocks,),
            in_specs=[
                pl.BlockSpec(memory_space=pl.ANY),
                pl.BlockSpec((None, None, d, ff2), wmap),
                pl.BlockSpec((1, 1, ff2), bmap),
                pl.BlockSpec((None, None, ff, d), wmap),
                pl.BlockSpec((1, 1, d), bmap),
            ],
            out_specs=pl.BlockSpec(memory_space=pl.ANY),
            scratch_shapes=[
                pltpu.VMEM((MOE_BLOCK, d), F32),
                pltpu.VMEM((MOE_BLOCK, d), F32),
                pltpu.VMEM((MOE_BLOCK, d), F32),
                pltpu.VMEM((MOE_BLOCK, d), F32),
                pltpu.VMEM((MOE_BLOCK, ff), BF16),
                pltpu.VMEM((d, ff2), BF16),
                pltpu.VMEM((ff, d), BF16),
                pltpu.SemaphoreType.DMA((2,)),
                pltpu.SemaphoreType.DMA((2,)),
            ],
        ),
        out_shape=jax.ShapeDtypeStruct((n_blocks * MOE_BLOCK, d), F32),
        compiler_params=_params("arbitrary"),
        name="moe_experts",
    )(block_e, dst_rows, x, w_gu_all, b_gu.reshape(n_exp, 1, ff2), w_dn_all, b_dn.reshape(n_exp, 1, d))


def _combine_kernel(x_ref, y0_ref, y1_ref, y2_ref, y3_ref, gate_ref, g_ref, b_ref, o_ref, obf_ref, *, alpha):
    gate = gate_ref[...]
    ffn = gate[:, 0:1] * y0_ref[...]
    for k, y_ref in enumerate((y1_ref, y2_ref, y3_ref), start=1):
        ffn = ffn + gate[:, k:k + 1] * y_ref[...]
    y = _layer_norm(alpha * x_ref[...] + ffn, g_ref[...], b_ref[...])
    o_ref[...] = y
    obf_ref[...] = y.astype(BF16)


def _combine_ln(x, ys, gate_pad, g, b, alpha, *, tm=256):
    t, d = x.shape
    tm = min(tm, t)
    nt = t // tm
    row = lambda i: (i, 0)
    fix = lambda i: (0, 0)
    y_specs = [pl.BlockSpec((tm, d), functools.partial(lambda i, k: (i + k * nt, 0), k=k)) for k in range(TOP_K)]
    return pl.pallas_call(
        functools.partial(_combine_kernel, alpha=alpha),
        grid=(nt,),
        in_specs=[pl.BlockSpec((tm, d), row)] + y_specs + [
            pl.BlockSpec((tm, LANES), row),
            pl.BlockSpec((1, d), fix),
            pl.BlockSpec((1, d), fix),
        ],
        out_specs=[pl.BlockSpec((tm, d), row), pl.BlockSpec((tm, d), row)],
        out_shape=[jax.ShapeDtypeStruct((t, d), F32), jax.ShapeDtypeStruct((t, d), BF16)],
        compiler_params=_params("parallel"),
        name="moe_combine_ln",
    )(x, ys, ys, ys, ys, gate_pad, g.reshape(1, d), b.reshape(1, d))


def _moe_layer(x, layer, rw, rb, w_gu_all, b_gu, w_dn_all, b_dn, g, b, alpha):
    t, d = x.shape
    n_exp = rw.shape[1]
    a = t * TOP_K
    gate_pad, idx_pad = _router(x, rw, rb)
    e_flat = idx_pad[:, :TOP_K].reshape(a)
    rank, counts = _expert_ranks(e_flat, n_exp)
    padded = (counts + MOE_BLOCK - 1) // MOE_BLOCK * MOE_BLOCK
    pend = jnp.cumsum(padded)
    pstart = pend - padded
    dest = pstart[e_flat] + rank
    n_slots = a + n_exp * MOE_BLOCK
    n_blocks = n_slots // MOE_BLOCK
    block_start = jnp.arange(n_blocks, dtype=jnp.int32) * MOE_BLOCK
    block_e = jnp.minimum(jnp.sum((block_start[:, None] >= pend[None, :]).astype(jnp.int32), axis=1), n_exp - 1)
    slot_a = _invert_dest(dest, pstart + counts, pend, n_slots)
    valid = slot_a >= 0
    pad_idx = jnp.cumsum(jnp.logical_not(valid).astype(jnp.int32)) - 1
    dst_rows = jnp.where(valid, (slot_a % TOP_K) * t + slot_a // TOP_K, a + pad_idx).astype(jnp.int32)
    ys = _expert_blocks(x, block_e, dst_rows.reshape(n_blocks, MOE_BLOCK), w_gu_all, w_dn_all, layer, b_gu, b_dn)
    return _combine_ln(x, ys, gate_pad, g, b, alpha)


def _mlstm_layer(x, x_bf, w_in_all, j, b_gates, norm_g, w_out_all, g, b, alpha, *, batch, seq):
    H = MLSTM_HEADS
    n_main = w_in_all.shape[2] - 4 * H
    proj = _matmul(x_bf, w_in_all, j, 0, n_main, jnp.zeros((n_main,), F32))
    gates_t = _mlstm_gates(x_bf, w_in_all[j, :, n_main:].T.astype(BF16), b_gates)
    h = _mlstm_scan(proj, gates_t, norm_g, batch=batch, seq=seq)
    return _matmul_res_ln(h, w_out_all, j, x, g, b, alpha)


def _gmlp_layer(x, x_bf, w_in_all, j, b_in, vn_g, vn_b, w_s, b_s, w_out_all, g, b, alpha):
    width = w_out_all.shape[1]
    u = _matmul(x_bf, w_in_all, j, 0, width, b_in[:width], act="gelu")
    vn = _matmul_gelu_ln(x_bf, w_in_all, j, 1, width, b_in[width:], vn_g, vn_b)
    y = _gmlp_spatial(u, vn, w_s, b_s)
    return _matmul_res_ln(y, w_out_all, j, x, g, b, alpha)


def kernel(x, mlstm_w_in, mlstm_b_gates, mlstm_norm_g, mlstm_w_out, gmlp_w_in, gmlp_b_in, gmlp_vnorm_g,
           gmlp_vnorm_b, gmlp_w_s, gmlp_b_s, gmlp_w_out, router_w, router_b, expert_w_gu, expert_b_gu,
           expert_w_down, expert_b_down, ln_g, ln_b):
    batch, seq, d = x.shape
    depth = ln_g.shape[0]
    alpha = float((2 * depth) ** 0.25)
    xf = x.reshape(batch * seq, d)
    x_bf = xf.astype(BF16)
    for layer in range(depth):
        j = layer // 2
        if layer % 2 == 0:
            xf, x_bf = _mlstm_layer(xf, x_bf, mlstm_w_in, j, mlstm_b_gates[j], mlstm_norm_g[j], mlstm_w_out,
                                    ln_g[layer, 0], ln_b[layer, 0], alpha, batch=batch, seq=seq)
        else:
            xf, x_bf = _gmlp_layer(xf, x_bf, gmlp_w_in, j, gmlp_b_in[j], gmlp_vnorm_g[j], gmlp_vnorm_b[j],
                                   gmlp_w_s[j], gmlp_b_s[j], gmlp_w_out,
                                   ln_g[layer, 0], ln_b[layer, 0], alpha)
        xf, x_bf = _moe_layer(xf, layer, router_w[layer], router_b[layer],
                              expert_w_gu, expert_b_gu[layer], expert_w_down, expert_b_down[layer],
                              ln_g[layer, 1], ln_b[layer, 1], alpha)
    return xf.reshape(batch, seq, d)
```

```python
import functools

import jax
import jax.numpy as jnp
from jax import lax
from jax.experimental import pallas as pl
from jax.experimental.pallas import tpu as pltpu

MLSTM_HEADS = 8
CHUNK = 128
GMLP_GROUPS = 8
TOP_K = 4
MOE_BLOCK = 128
SWIGLU_LIMIT = 7.0
SWIGLU_ALPHA = 1.702
LN_EPS = 1e-5
LANES = 128
VMEM_LIMIT = 56 * 1024 * 1024
DMA_THREADS = 2

F32 = jnp.float32
BF16 = jnp.bfloat16


def _params(*sem):
    return pltpu.CompilerParams(dimension_semantics=sem, vmem_limit_bytes=VMEM_LIMIT)


def _gelu(x):
    return 0.5 * x * (1.0 + lax.erf(x * (2.0 ** -0.5)))


def _layer_norm(z, g, b):
    mu = jnp.mean(z, axis=-1, keepdims=True)
    zc = z - mu
    var = jnp.mean(zc * zc, axis=-1, keepdims=True)
    return zc * lax.rsqrt(var + LN_EPS) * g + b


def _mm_kernel(x_ref, w_ref, b_ref, o_ref, wbf_ref, *, act):
    @pl.when(pl.program_id(1) == 0)
    def _():
        wbf_ref[...] = w_ref[...].astype(BF16)

    acc = jnp.dot(x_ref[...], wbf_ref[...], preferred_element_type=F32)
    acc = acc + b_ref[...]
    if act == "gelu":
        acc = _gelu(acc)
    o_ref[...] = acc.astype(o_ref.dtype)


def _matmul(x, w_all, layer, col0, n, b, *, act=None, out_dtype=BF16, tm=1024, tn=1024):
    m, k = x.shape
    tm, tn = min(tm, m), min(tn, n)
    assert m % tm == 0 and n % tn == 0 and col0 % tn == 0
    c0 = col0 // tn
    return pl.pallas_call(
        functools.partial(_mm_kernel, act=act),
        grid=(n // tn, m // tm),
        in_specs=[
            pl.BlockSpec((tm, k), lambda j, i: (i, 0)),
            pl.BlockSpec((None, k, tn), lambda j, i: (layer, 0, c0 + j)),
            pl.BlockSpec((1, tn), lambda j, i: (0, j)),
        ],
        out_specs=pl.BlockSpec((tm, tn), lambda j, i: (i, j)),
        out_shape=jax.ShapeDtypeStruct((m, n), out_dtype),
        scratch_shapes=[pltpu.VMEM((k, tn), BF16)],
        compiler_params=_params("parallel", "arbitrary"),
        name="matmul_bias_act",
    )(x, w_all, b.reshape(1, n).astype(F32))


def _mm_ln_kernel(*refs, alpha, residual, gelu, two_out):
    wbf_ref = refs[-1]
    refs = refs[:-1]
    if residual:
        h_ref, w_ref, res_ref, g_ref, b_ref = refs[:5]
        outs = refs[5:]
    else:
        h_ref, w_ref, bias_ref, g_ref, b_ref = refs[:5]
        outs = refs[5:]

    @pl.when(pl.program_id(0) == 0)
    def _():
        wbf_ref[...] = w_ref[...].astype(BF16)

    acc = jnp.dot(h_ref[...], wbf_ref[...], preferred_element_type=F32)
    if residual:
        z = alpha * res_ref[...] + acc
    else:
        z = acc + bias_ref[...]
    if gelu:
        z = _gelu(z)
    y = _layer_norm(z, g_ref[...], b_ref[...])
    if two_out:
        outs[0][...] = y
        outs[1][...] = y.astype(BF16)
    else:
        outs[0][...] = y.astype(outs[0].dtype)


def _weight_spec(k, n, layer, colblk):
    return pl.BlockSpec((None, k, n), lambda i: (layer, 0, colblk), pipeline_mode=pl.Buffered(1))


def _matmul_res_ln(h, w_all, layer, res, g, b, alpha, *, tm=256):
    m, k = h.shape
    n = w_all.shape[2]
    tm = min(tm, m)
    assert m % tm == 0
    row = lambda i: (i, 0)
    fix = lambda i: (0, 0)
    return pl.pallas_call(
        functools.partial(_mm_ln_kernel, alpha=alpha, residual=True, gelu=False, two_out=True),
        grid=(m // tm,),
        in_specs=[
            pl.BlockSpec((tm, k), row),
            _weight_spec(k, n, layer, 0),
            pl.BlockSpec((tm, n), row),
            pl.BlockSpec((1, n), fix),
            pl.BlockSpec((1, n), fix),
        ],
        out_specs=[pl.BlockSpec((tm, n), row), pl.BlockSpec((tm, n), row)],
        out_shape=[jax.ShapeDtypeStruct((m, n), F32), jax.ShapeDtypeStruct((m, n), BF16)],
        scratch_shapes=[pltpu.VMEM((k, n), BF16)],
        compiler_params=_params("arbitrary"),
        name="matmul_residual_ln",
    )(h, w_all, res, g.reshape(1, n), b.reshape(1, n))


def _matmul_gelu_ln(h, w_all, layer, colblk, n, bias, g, b, *, tm=256):
    m, k = h.shape
    tm = min(tm, m)
    assert m % tm == 0
    row = lambda i: (i, 0)
    fix = lambda i: (0, 0)
    return pl.pallas_call(
        functools.partial(_mm_ln_kernel, alpha=None, residual=False, gelu=True, two_out=False),
        grid=(m // tm,),
        in_specs=[
            pl.BlockSpec((tm, k), row),
            _weight_spec(k, n, layer, colblk),
            pl.BlockSpec((1, n), fix),
            pl.BlockSpec((1, n), fix),
            pl.BlockSpec((1, n), fix),
        ],
        out_specs=pl.BlockSpec((tm, n), row),
        out_shape=jax.ShapeDtypeStruct((m, n), BF16),
        scratch_shapes=[pltpu.VMEM((k, n), BF16)],
        compiler_params=_params("arbitrary"),
        name="matmul_gelu_ln",
    )(h, w_all, bias.reshape(1, n), g.reshape(1, n), b.reshape(1, n))


def _gates_kernel(w_ref, x_ref, b_ref, o_ref, *, heads):
    g = lax.dot_general(w_ref[...].astype(BF16), x_ref[...], (((0,), (1,)), ((), ())),
                        preferred_element_type=F32)
    g = g + b_ref[...]
    r = lax.broadcasted_iota(jnp.int32, g.shape, 0)
    is_forget = ((r >= heads) & (r < 2 * heads)) | (r >= 3 * heads)
    log_sig = jnp.minimum(g, 0.0) - jnp.log1p(jnp.exp(-jnp.abs(g)))
    o_ref[...] = jnp.where(is_forget, log_sig, g)


def _mlstm_gates(x_bf, w_gates, b_gates, *, tm=1024):
    t, k = x_bf.shape
    ng = w_gates.shape[1]
    tm = min(tm, t)
    assert t % tm == 0
    return pl.pallas_call(
        functools.partial(_gates_kernel, heads=ng // 4),
        grid=(t // tm,),
        in_specs=[
            pl.BlockSpec((k, ng), lambda i: (0, 0)),
            pl.BlockSpec((tm, k), lambda i: (i, 0)),
            pl.BlockSpec((ng, 1), lambda i: (0, 0)),
        ],
        out_specs=pl.BlockSpec((ng, tm), lambda i: (0, i)),
        out_shape=jax.ShapeDtypeStruct((ng, t), F32),
        compiler_params=_params("parallel"),
        name="mlstm_gates",
    )(w_gates, x_bf, b_gates.reshape(ng, 1).astype(F32))


def _mlstm_kernel(q_ref, k_ref, v_ref, o_ref, gt_ref, ng_ref, out_ref, hfw_ref, c_ref, *, nc, scale):
    L = CHUNK
    dk = q_ref.shape[1]
    row_i = lax.broadcasted_iota(jnp.int32, (L, L), 0)
    col_i = lax.broadcasted_iota(jnp.int32, (L, L), 1)
    eye = row_i == col_i

    def chunk_step(c, carry, reverse):
        n, m = carry
        off = pl.multiple_of(c * L, L)
        qc = q_ref[pl.ds(off, L), :]
        kc = k_ref[pl.ds(off, L), :]
        vc = v_ref[pl.ds(off, L), :]
        rows = gt_ref[0, 0, c]
        r0 = 2 if reverse else 0
        i_row = rows[r0:r0 + 1, :]
        f_row = rows[r0 + 1:r0 + 2, :]
        mask = (col_i >= row_i) if reverse else (col_i <= row_i)
        b_col = jnp.sum(jnp.where(mask, f_row, 0.0), axis=1, keepdims=True)
        b_row = jnp.sum(jnp.where(eye, b_col, 0.0), axis=0, keepdims=True)
        i_col = jnp.sum(jnp.where(eye, i_row, 0.0), axis=1, keepdims=True)
        dmat = jnp.where(mask, b_col - b_row + i_row, -jnp.inf)
        inter = b_col + m
        m_t = jnp.maximum(inter, jnp.max(dmat, axis=1, keepdims=True))
        p = jnp.exp(dmat - m_t)
        sqk = lax.dot_general(qc, kc, (((1,), (1,)), ((), ())), preferred_element_type=F32)
        s = sqk * (scale * p)
        isc = jnp.exp(inter - m_t)
        qn = jnp.sum(qc.astype(F32) * n, axis=1, keepdims=True)
        den = jnp.sum(s, axis=1, keepdims=True) + isc * (scale * qn)
        c_old = c_ref[...]
        num = jnp.dot(s.astype(BF16), vc, preferred_element_type=F32)
        num = num + (isc * scale) * jnp.dot(qc, c_old.astype(BF16), preferred_element_type=F32)
        h = num / jnp.maximum(jnp.abs(den), jnp.exp(-m_t))
        b_last = jnp.sum(f_row, axis=1, keepdims=True)
        g_col = b_last - b_col + i_col
        m_new = jnp.maximum(b_last + m, jnp.max(g_col, axis=0, keepdims=True))
        decay = jnp.exp(b_last + m - m_new)
        wk = jnp.exp(g_col - m_new)
        kw = kc.astype(F32) * wk
        c_ref[...] = decay * c_old + lax.dot_general(
            kw.astype(BF16), vc, (((0,), (0,)), ((), ())), preferred_element_type=F32)
        n_new = decay * n + jnp.sum(kw, axis=0, keepdims=True)
        return h, off, (n_new, m_new)

    init = (jnp.zeros((1, dk), F32), jnp.zeros((1, 1), F32))

    def fw_body(c, carry):
        h, off, carry = chunk_step(c, carry, False)
        hfw_ref[pl.ds(off, L), :] = h
        return carry

    c_ref[...] = jnp.zeros_like(c_ref)
    lax.fori_loop(0, nc, fw_body, init)

    def bw_body(i, carry):
        h, off, carry = chunk_step(nc - 1 - i, carry, True)
        tot = hfw_ref[pl.ds(off, L), :] + h
        mu = jnp.mean(tot, axis=1, keepdims=True)
        hc = tot - mu
        var = jnp.mean(hc * hc, axis=1, keepdims=True)
        hn = hc * lax.rsqrt(var + LN_EPS) * ng_ref[...]
        og = jax.nn.sigmoid(o_ref[pl.ds(off, L), :].astype(F32))
        out_ref[pl.ds(off, L), :] = (og * hn).astype(out_ref.dtype)
        return carry

    c_ref[...] = jnp.zeros_like(c_ref)
    lax.fori_loop(0, nc, bw_body, init)


def _mlstm_scan(proj, gates_t, norm_g, *, batch, seq):
    H = MLSTM_HEADS
    L = CHUNK
    v_w = norm_g.shape[0]
    dv = v_w // H
    dk = (proj.shape[1] - 2 * v_w) // (2 * H)
    nc = seq // L
    assert seq % L == 0 and dv % LANES == 0 and dk % LANES == 0
    gt = gates_t.reshape(4, H, batch, nc, L).transpose(2, 1, 3, 0, 4)
    gt = jnp.pad(gt, ((0, 0), (0, 0), (0, 0), (0, 4), (0, 0)))
    k0 = H
    v0 = 2 * H * dk // dv
    o0 = v0 + H
    return pl.pallas_call(
        functools.partial(_mlstm_kernel, nc=nc, scale=float(dk) ** -0.5),
        grid=(batch, H),
        in_specs=[
            pl.BlockSpec((seq, dk), lambda b, h: (b, h)),
            pl.BlockSpec((seq, dk), lambda b, h: (b, k0 + h)),
            pl.BlockSpec((seq, dv), lambda b, h: (b, v0 + h)),
            pl.BlockSpec((seq, dv), lambda b, h: (b, o0 + h)),
            pl.BlockSpec((1, 1, nc, 8, L), lambda b, h: (b, h, 0, 0, 0)),
            pl.BlockSpec((1, dv), lambda b, h: (0, h)),
        ],
        out_specs=pl.BlockSpec((seq, dv), lambda b, h: (b, h)),
        out_shape=jax.ShapeDtypeStruct((batch * seq, v_w), BF16),
        scratch_shapes=[pltpu.VMEM((seq, dv), F32), pltpu.VMEM((dk, dv), F32)],
        compiler_params=_params("parallel", "parallel"),
        name="mlstm_scan",
    )(proj, proj, proj, proj, gt, norm_g.reshape(1, v_w).astype(F32))


def _spatial_kernel(ws_ref, bs_ref, v_ref, u_ref, y_ref, *, groups):
    gd = v_ref.shape[1] // groups
    for g in range(groups):
        cols = slice(g * gd, (g + 1) * gd)
        s = jnp.dot(ws_ref[g], v_ref[:, cols], preferred_element_type=F32) + bs_ref[g]
        y_ref[:, cols] = (u_ref[:, cols].astype(F32) * s).astype(y_ref.dtype)


def _gmlp_spatial(u, vn, w_s, b_s):
    t, width = u.shape
    groups, L, _ = w_s.shape
    blk = pl.BlockSpec((L, width), lambda c: (c, 0))
    return pl.pallas_call(
        functools.partial(_spatial_kernel, groups=groups),
        grid=(t // L,),
        in_specs=[
            pl.BlockSpec((groups, L, L), lambda c: (0, 0, 0)),
            pl.BlockSpec((groups, L, 1), lambda c: (0, 0, 0)),
            blk, blk,
        ],
        out_specs=blk,
        out_shape=jax.ShapeDtypeStruct((t, width), BF16),
        compiler_params=_params("parallel"),
        name="gmlp_spatial",
    )(w_s.astype(BF16), b_s.reshape(groups, L, 1).astype(F32), vn, u)


def _router_kernel(x_ref, w_ref, b_ref, gate_ref, idx_ref, *, n_experts):
    x = x_ref[...]
    w = w_ref[...]
    xh = x.astype(BF16)
    xl = (x - xh.astype(F32)).astype(BF16)
    wh = w.astype(BF16)
    wl = (w - wh.astype(F32)).astype(BF16)
    logits = (jnp.dot(xh, wh, preferred_element_type=F32) + jnp.dot(xh, wl, preferred_element_type=F32)
              + jnp.dot(xl, wh, preferred_element_type=F32)) + b_ref[...]
    lane = lax.broadcasted_iota(jnp.int32, logits.shape, 1)
    cur = jnp.where(lane < n_experts, logits, -jnp.inf)
    vals, idxs = [], []
    for _ in range(TOP_K):
        mx = jnp.max(cur, axis=1, keepdims=True)
        ix = jnp.min(jnp.where(cur == mx, lane, LANES), axis=1, keepdims=True)
        vals.append(mx)
        idxs.append(ix)
        cur = jnp.where(lane == ix, -jnp.inf, cur)
    exps = [jnp.exp(v - vals[0]) for v in vals]
    tot = exps[0]
    for e in exps[1:]:
        tot = tot + e
    gate = jnp.zeros(logits.shape, F32)
    idx = jnp.zeros(logits.shape, jnp.int32)
    for k in range(TOP_K):
        gate = jnp.where(lane == k, exps[k] / tot, gate)
        idx = jnp.where(lane == k, idxs[k], idx)
    gate_ref[...] = gate
    idx_ref[...] = idx


def _router(x, rw, rb, *, tm=512):
    t, d = x.shape
    e = rw.shape[1]
    assert e <= LANES
    tm = min(tm, t)
    w_pad = jnp.pad(rw.astype(F32), ((0, 0), (0, LANES - e)))
    b_pad = jnp.pad(rb.astype(F32), (0, LANES - e)).reshape(1, LANES)
    return pl.pallas_call(
        functools.partial(_router_kernel, n_experts=e),
        grid=(t // tm,),
        in_specs=[
            pl.BlockSpec((tm, d), lambda i: (i, 0)),
            pl.BlockSpec((d, LANES), lambda i: (0, 0)),
            pl.BlockSpec((1, LANES), lambda i: (0, 0)),
        ],
        out_specs=[pl.BlockSpec((tm, LANES), lambda i: (i, 0))] * 2,
        out_shape=[jax.ShapeDtypeStruct((t, LANES), F32), jax.ShapeDtypeStruct((t, LANES), jnp.int32)],
        compiler_params=_params("parallel"),
        name="moe_router",
    )(x, w_pad, b_pad)


RANK_BLOCK = 512


def _rank_kernel(e_ref, rank_ref, cnt_ref, carry_ref):
    @pl.when(pl.program_id(0) == 0)
    def _():
        carry_ref[...] = jnp.zeros_like(carry_ref)

    e_row = e_ref[0]
    ex = lax.broadcasted_iota(jnp.int32, (LANES, RANK_BLOCK), 0)
    onehot = e_row == ex
    a_i = lax.broadcasted_iota(jnp.int32, (RANK_BLOCK, RANK_BLOCK), 0)
    b_i = lax.broadcasted_iota(jnp.int32, (RANK_BLOCK, RANK_BLOCK), 1)
    before = (a_i < b_i).astype(BF16)
    within = jnp.dot(onehot.astype(BF16), before, preferred_element_type=F32)
    carry = carry_ref[...][:, :1]
    rank = jnp.sum(jnp.where(onehot, within + carry, 0.0), axis=0, keepdims=True)
    rank_ref[0] = rank.astype(jnp.int32)
    new = carry + jnp.sum(onehot.astype(F32), axis=1, keepdims=True)
    carry_ref[...] = jnp.broadcast_to(new, carry_ref.shape)
    cnt_ref[...] = jnp.broadcast_to(new, cnt_ref.shape).astype(jnp.int32)


def _expert_ranks(e_flat, n_experts):
    a = e_flat.shape[0]
    assert a % RANK_BLOCK == 0 and n_experts <= LANES
    nb = a // RANK_BLOCK
    rank, cnt = pl.pallas_call(
        _rank_kernel,
        grid=(nb,),
        in_specs=[pl.BlockSpec((1, 1, RANK_BLOCK), lambda i: (i, 0, 0))],
        out_specs=[pl.BlockSpec((1, 1, RANK_BLOCK), lambda i: (i, 0, 0)),
                   pl.BlockSpec((LANES, LANES), lambda i: (0, 0))],
        out_shape=[jax.ShapeDtypeStruct((nb, 1, RANK_BLOCK), jnp.int32),
                   jax.ShapeDtypeStruct((LANES, LANES), jnp.int32)],
        scratch_shapes=[pltpu.VMEM((LANES, LANES), F32)],
        compiler_params=_params("arbitrary"),
        name="moe_rank",
    )(e_flat.reshape(nb, 1, RANK_BLOCK))
    return rank.reshape(a), cnt[:n_experts, 0]


INVERT_CHUNK = 4096


def _invert_kernel(dest_ref, lo_ref, hi_ref, out_ref, *, chunk, n_exp, n_slots):
    step = pl.program_id(0)

    def fill_range(lo, hi):
        def fill(s, c):
            out_ref[s] = -1
            return c
        lax.fori_loop(lo, hi, fill, 0)

    @pl.when(step == 0)
    def _():
        for e in range(n_exp):
            fill_range(lo_ref[e], hi_ref[e])
        fill_range(hi_ref[n_exp - 1], n_slots)

    @pl.when(step > 0)
    def _():
        base = (step - 1) * chunk

        def put(i, c):
            a = base + i
            out_ref[dest_ref[a]] = a
            return c
        lax.fori_loop(0, chunk, put, 0, unroll=8)


def _invert_dest(dest, pad_lo, pad_hi, n_slots):
    a = dest.shape[0]
    chunk = min(INVERT_CHUNK, a)
    assert a % chunk == 0
    smem = pl.BlockSpec(memory_space=pltpu.SMEM)
    return pl.pallas_call(
        functools.partial(_invert_kernel, chunk=chunk, n_exp=pad_lo.shape[0], n_slots=n_slots),
        grid=(a // chunk + 1,),
        in_specs=[smem, smem, smem],
        out_specs=smem,
        out_shape=jax.ShapeDtypeStruct((n_slots,), jnp.int32),
        compiler_params=_params("arbitrary"),
        name="moe_invert",
    )(dest, pad_lo, pad_hi)


def _expert_kernel(be_ref, dst_ref, x_hbm, wgu_ref, bgu_ref, wdn_ref, bdn_ref, out_hbm,
                   xbuf0, xbuf1, ybuf0, ybuf1, hbuf, wgu_bf, wdn_bf, gsem, ssem, *, n_tokens, n_blocks, ff):
    j = pl.program_id(0)
    R = MOE_BLOCK
    xbufs = (xbuf0, xbuf1)
    ybufs = (ybuf0, ybuf1)

    @pl.when((j == 0) | (be_ref[j] != be_ref[jnp.maximum(j - 1, 0)]))
    def _():
        wgu_bf[...] = wgu_ref[...].astype(BF16)
        wdn_bf[...] = wdn_ref[...].astype(BF16)

    def start_gather(blk, s):
        for r in range(R):
            tok = dst_ref[blk, r] & (n_tokens - 1)
            pltpu.make_async_copy(x_hbm.at[pl.ds(tok, 1)], xbufs[s].at[pl.ds(r, 1)], gsem.at[s]).start(priority=r % DMA_THREADS)

    def wait_gather(s):
        pltpu.make_async_copy(x_hbm.at[pl.ds(0, R)], xbufs[s], gsem.at[s]).wait()

    def start_scatter(blk, s):
        for r in range(R):
            pltpu.make_async_copy(ybufs[s].at[pl.ds(r, 1)], out_hbm.at[pl.ds(dst_ref[blk, r], 1)],
                                  ssem.at[s]).start(priority=r % DMA_THREADS)

    def wait_scatter(s):
        pltpu.make_async_copy(ybufs[s], out_hbm.at[pl.ds(0, R)], ssem.at[s]).wait()

    def up_proj(s):
        xb = xbufs[s][...].astype(BF16)
        hgu = jnp.dot(xb, wgu_bf[...], preferred_element_type=F32) + bgu_ref[0]
        g = jnp.minimum(hgu[:, :ff], SWIGLU_LIMIT)
        u = jnp.clip(hgu[:, ff:], -SWIGLU_LIMIT, SWIGLU_LIMIT)
        hbuf[...] = ((u + 1.0) * (g * jax.nn.sigmoid(g * SWIGLU_ALPHA))).astype(BF16)

    def down_proj(s):
        ybufs[s][...] = jnp.dot(hbuf[...], wdn_bf[...], preferred_element_type=F32) + bdn_ref[0]

    @pl.when(j == 0)
    def _():
        start_gather(0, 0)
        wait_gather(0)
        start_gather(1, 1)
        up_proj(0)
        down_proj(0)

    for s in (0, 1):
        @pl.when((j > 0) & (lax.rem(j, 2) == s))
        def _(s=s):
            wait_gather(s)
            start_gather(jnp.minimum(j + 1, n_blocks - 1), 1 - s)
            up_proj(s)

            @pl.when(j >= 2)
            def _():
                wait_scatter(s)

            start_scatter(j - 1, 1 - s)
            down_proj(s)

    @pl.when(j == n_blocks - 1)
    def _():
        s_last = (n_blocks - 1) % 2
        start_scatter(n_blocks - 1, s_last)
        wait_gather(1 - s_last)
        wait_scatter(1 - s_last)
        wait_scatter(s_last)


def _expert_blocks(x, block_e, dst_rows, w_gu_all, w_dn_all, layer, b_gu, b_dn):
    t, d = x.shape
    n_blocks = block_e.shape[0]
    _, n_exp, _, ff2 = w_gu_all.shape
    ff = ff2 // 2
    assert t & (t - 1) == 0, "token count must be a power of two"
    assert n_blocks >= 2
    wmap = lambda j, be, ds: (layer, be[j], 0, 0)
    bmap = lambda j, be, ds: (be[j], 0, 0)
    return pl.pallas_call(
        functools.partial(_expert_kernel, n_tokens=t, n_blocks=n_blocks, ff=ff),
        grid_spec=pltpu.PrefetchScalarGridSpec(
            num_scalar_prefetch=2,
            grid=(n_blocks,),
            in_specs=[
                pl.BlockSpec(memory_space=pl.ANY),
                pl.BlockSpec((None, None, d, ff2), wmap),
                pl.BlockSpec((1, 1, ff2), bmap),
                pl.BlockSpec((None, None, ff, d), wmap),
                pl.BlockSpec((1, 1, d), bmap),
            ],
            out_specs=pl.BlockSpec(memory_space=pl.ANY),
            scratch_shapes=[
                pltpu.VMEM((MOE_BLOCK, d), F32),
                pltpu.VMEM((MOE_BLOCK, d), F32),
                pltpu.VMEM((MOE_BLOCK, d), F32),
                pltpu.VMEM((MOE_BLOCK, d), F32),
                pltpu.VMEM((MOE_BLOCK, ff), BF16),
                pltpu.VMEM((d, ff2), BF16),
                pltpu.VMEM((ff, d), BF16),
                pltpu.SemaphoreType.DMA((2,)),
                pltpu.SemaphoreType.DMA((2,)),
            ],
        ),
        out_shape=jax.ShapeDtypeStruct((n_blocks * MOE_BLOCK, d), F32),
        compiler_params=_params("arbitrary"),
        name="moe_experts",
    )(block_e, dst_rows, x, w_gu_all, b_gu.reshape(n_exp, 1, ff2), w_dn_all, b_dn.reshape(n_exp, 1, d))


def _combine_kernel(x_ref, y0_ref, y1_ref, y2_ref, y3_ref, gate_ref, g_ref, b_ref, o_ref, obf_ref, *, alpha):
    gate = gate_ref[...]
    ffn = gate[:, 0:1] * y0_ref[...]
    for k, y_ref in enumerate((y1_ref, y2_ref, y3_ref), start=1):
        ffn = ffn + gate[:, k:k + 1] * y_ref[...]
    y = _layer_norm(alpha * x_ref[...] + ffn, g_ref[...], b_ref[...])
    o_ref[...] = y
    obf_ref[...] = y.astype(BF16)


def _combine_ln(x, ys, gate_pad, g, b, alpha, *, tm=256):
    t, d = x.shape
    tm = min(tm, t)
    nt = t // tm
    row = lambda i: (i, 0)
    fix = lambda i: (0, 0)
    y_specs = [pl.BlockSpec((tm, d), functools.partial(lambda i, k: (i + k * nt, 0), k=k)) for k in range(TOP_K)]
    return pl.pallas_call(
        functools.partial(_combine_kernel, alpha=alpha),
        grid=(nt,),
        in_specs=[pl.BlockSpec((tm, d), row)] + y_specs + [
            pl.BlockSpec((tm, LANES), row),
            pl.BlockSpec((1, d), fix),
            pl.BlockSpec((1, d), fix),
        ],
        out_specs=[pl.BlockSpec((tm, d), row), pl.BlockSpec((tm, d), row)],
        out_shape=[jax.ShapeDtypeStruct((t, d), F32), jax.ShapeDtypeStruct((t, d), BF16)],
        compiler_params=_params("parallel"),
        name="moe_combine_ln",
    )(x, ys, ys, ys, ys, gate_pad, g.reshape(1, d), b.reshape(1, d))


def _moe_layer(x, layer, rw, rb, w_gu_all, b_gu, w_dn_all, b_dn, g, b, alpha):
    t, d = x.shape
    n_exp = rw.shape[1]
    a = t * TOP_K
    gate_pad, idx_pad = _router(x, rw, rb)
    e_flat = idx_pad[:, :TOP_K].reshape(a)
    rank, counts = _expert_ranks(e_flat, n_exp)
    padded = (counts + MOE_BLOCK - 1) // MOE_BLOCK * MOE_BLOCK
    pend = jnp.cumsum(padded)
    pstart = pend - padded
    dest = pstart[e_flat] + rank
    n_slots = a + n_exp * MOE_BLOCK
    n_blocks = n_slots // MOE_BLOCK
    block_start = jnp.arange(n_blocks, dtype=jnp.int32) * MOE_BLOCK
    block_e = jnp.minimum(jnp.sum((block_start[:, None] >= pend[None, :]).astype(jnp.int32), axis=1), n_exp - 1)
    slot_a = _invert_dest(dest, pstart + counts, pend, n_slots)
    valid = slot_a >= 0
    pad_idx = jnp.cumsum(jnp.logical_not(valid).astype(jnp.int32)) - 1
    dst_rows = jnp.where(valid, (slot_a % TOP_K) * t + slot_a // TOP_K, a + pad_idx).astype(jnp.int32)
    ys = _expert_blocks(x, block_e, dst_rows.reshape(n_blocks, MOE_BLOCK), w_gu_all, w_dn_all, layer, b_gu, b_dn)
    return _combine_ln(x, ys, gate_pad, g, b, alpha)


def _mlstm_layer(x, x_bf, w_in_all, j, b_gates, norm_g, w_out_all, g, b, alpha, *, batch, seq):
    H = MLSTM_HEADS
    n_main = w_in_all.shape[2] - 4 * H
    proj = _matmul(x_bf, w_in_all, j, 0, n_main, jnp.zeros((n_main,), F32))
    gates_t = _mlstm_gates(x_bf, w_in_all[j, :, n_main:], b_gates)
    h = _mlstm_scan(proj, gates_t, norm_g, batch=batch, seq=seq)
    return _matmul_res_ln(h, w_out_all, j, x, g, b, alpha)


def _gmlp_layer(x, x_bf, w_in_all, j, b_in, vn_g, vn_b, w_s, b_s, w_out_all, g, b, alpha):
    width = w_out_all.shape[1]
    u = _matmul(x_bf, w_in_all, j, 0, width, b_in[:width], act="gelu")
    vn = _matmul_gelu_ln(x_bf, w_in_all, j, 1, width, b_in[width:], vn_g, vn_b)
    y = _gmlp_spatial(u, vn, w_s, b_s)
    return _matmul_res_ln(y, w_out_all, j, x, g, b, alpha)


def kernel(x, mlstm_w_in, mlstm_b_gates, mlstm_norm_g, mlstm_w_out, gmlp_w_in, gmlp_b_in, gmlp_vnorm_g,
           gmlp_vnorm_b, gmlp_w_s, gmlp_b_s, gmlp_w_out, router_w, router_b, expert_w_gu, expert_b_gu,
           expert_w_down, expert_b_down, ln_g, ln_b):
    batch, seq, d = x.shape
    depth = ln_g.shape[0]
    alpha = float((2 * depth) ** 0.25)
    xf = x.reshape(batch * seq, d)
    x_bf = xf.astype(BF16)
    for layer in range(depth):
        j = layer // 2
        if layer % 2 == 0:
            xf, x_bf = _mlstm_layer(xf, x_bf, mlstm_w_in, j, mlstm_b_gates[j], mlstm_norm_g[j], mlstm_w_out,
                                    ln_g[layer, 0], ln_b[layer, 0], alpha, batch=batch, seq=seq)
        else:
            xf, x_bf = _gmlp_layer(xf, x_bf, gmlp_w_in, j, gmlp_b_in[j], gmlp_vnorm_g[j], gmlp_vnorm_b[j],
                                   gmlp_w_s[j], gmlp_b_s[j], gmlp_w_out,
                                   ln_g[layer, 0], ln_b[layer, 0], alpha)
        xf, x_bf = _moe_layer(xf, layer, router_w[layer], router_b[layer],
                              expert_w_gu, expert_b_gu[layer], expert_w_down, expert_b_down[layer],
                              ln_g[layer, 1], ln_b[layer, 1], alpha)
    return xf.reshape(batch, seq, d)
```

```python
import functools

import jax
import jax.numpy as jnp
from jax import lax
from jax.experimental import pallas as pl
from jax.experimental.pallas import tpu as pltpu

MLSTM_HEADS = 8
CHUNK = 128
GMLP_GROUPS = 8
TOP_K = 4
MOE_BLOCK = 128
SWIGLU_LIMIT = 7.0
SWIGLU_ALPHA = 1.702
LN_EPS = 1e-5
LANES = 128
VMEM_LIMIT = 56 * 1024 * 1024

F32 = jnp.float32
BF16 = jnp.bfloat16


def _params(*sem):
    return pltpu.CompilerParams(dimension_semantics=sem, vmem_limit_bytes=VMEM_LIMIT)


def _gelu(x):
    return 0.5 * x * (1.0 + lax.erf(x * (2.0 ** -0.5)))


def _layer_norm(z, g, b):
    mu = jnp.mean(z, axis=-1, keepdims=True)
    zc = z - mu
    var = jnp.mean(zc * zc, axis=-1, keepdims=True)
    return zc * lax.rsqrt(var + LN_EPS) * g + b


def _to_slab(ref, y):
    rows, d = y.shape
    nch = d // LANES
    for c in range(nch):
        ref[pl.ds(c, rows, stride=nch), :] = y[:, c * LANES:(c + 1) * LANES]


def _from_slab(ref, rows):
    nch = ref.shape[0] // rows
    return jnp.concatenate([ref[pl.ds(c, rows, stride=nch), :] for c in range(nch)], axis=1)


def _mm_kernel(x_ref, w_ref, b_ref, o_ref, wbf_ref, *, act):
    @pl.when(pl.program_id(1) == 0)
    def _():
        wbf_ref[...] = w_ref[...].astype(BF16)

    acc = jnp.dot(x_ref[...], wbf_ref[...], preferred_element_type=F32)
    acc = acc + b_ref[...]
    if act == "gelu":
        acc = _gelu(acc)
    o_ref[...] = acc.astype(o_ref.dtype)


def _matmul(x, w_all, layer, col0, n, b, *, act=None, out_dtype=BF16, tm=1024, tn=1024):
    m, k = x.shape
    tm, tn = min(tm, m), min(tn, n)
    assert m % tm == 0 and n % tn == 0 and col0 % tn == 0
    c0 = col0 // tn
    return pl.pallas_call(
        functools.partial(_mm_kernel, act=act),
        grid=(n // tn, m // tm),
        in_specs=[
            pl.BlockSpec((tm, k), lambda j, i: (i, 0)),
            pl.BlockSpec((None, k, tn), lambda j, i: (layer, 0, c0 + j)),
            pl.BlockSpec((1, tn), lambda j, i: (0, j)),
        ],
        out_specs=pl.BlockSpec((tm, tn), lambda j, i: (i, j)),
        out_shape=jax.ShapeDtypeStruct((m, n), out_dtype),
        scratch_shapes=[pltpu.VMEM((k, tn), BF16)],
        compiler_params=_params("parallel", "arbitrary"),
        name="matmul_bias_act",
    )(x, w_all, b.reshape(1, n).astype(F32))


def _mm_ln_kernel(*refs, alpha, residual, gelu, two_out):
    wbf_ref = refs[-1]
    refs = refs[:-1]
    if residual:
        h_ref, w_ref, res_ref, g_ref, b_ref = refs[:5]
        outs = refs[5:]
    else:
        h_ref, w_ref, bias_ref, g_ref, b_ref = refs[:5]
        outs = refs[5:]

    @pl.when(pl.program_id(0) == 0)
    def _():
        wbf_ref[...] = w_ref[...].astype(BF16)

    acc = jnp.dot(h_ref[...], wbf_ref[...], preferred_element_type=F32)
    if residual:
        z = alpha * res_ref[...] + acc
    else:
        z = acc + bias_ref[...]
    if gelu:
        z = _gelu(z)
    y = _layer_norm(z, g_ref[...], b_ref[...])
    if two_out:
        outs[0][...] = y
        outs[1][...] = y.astype(BF16)
        _to_slab(outs[2], y)
    else:
        outs[0][...] = y.astype(outs[0].dtype)


def _weight_spec(k, n, layer, colblk):
    return pl.BlockSpec((None, k, n), lambda i: (layer, 0, colblk), pipeline_mode=pl.Buffered(1))


def _matmul_res_ln(h, w_all, layer, res, g, b, alpha, *, tm=256):
    m, k = h.shape
    n = w_all.shape[2]
    tm = min(tm, m)
    assert m % tm == 0
    row = lambda i: (i, 0)
    fix = lambda i: (0, 0)
    return pl.pallas_call(
        functools.partial(_mm_ln_kernel, alpha=alpha, residual=True, gelu=False, two_out=True),
        grid=(m // tm,),
        in_specs=[
            pl.BlockSpec((tm, k), row),
            _weight_spec(k, n, layer, 0),
            pl.BlockSpec((tm, n), row),
            pl.BlockSpec((1, n), fix),
            pl.BlockSpec((1, n), fix),
        ],
        out_specs=[pl.BlockSpec((tm, n), row), pl.BlockSpec((tm, n), row),
                   pl.BlockSpec((tm * (n // LANES), LANES), row)],
        out_shape=[jax.ShapeDtypeStruct((m, n), F32), jax.ShapeDtypeStruct((m, n), BF16),
                   jax.ShapeDtypeStruct((m * (n // LANES), LANES), F32)],
        scratch_shapes=[pltpu.VMEM((k, n), BF16)],
        compiler_params=_params("arbitrary"),
        name="matmul_residual_ln",
    )(h, w_all, res, g.reshape(1, n), b.reshape(1, n))


def _matmul_gelu_ln(h, w_all, layer, colblk, n, bias, g, b, *, tm=256):
    m, k = h.shape
    tm = min(tm, m)
    assert m % tm == 0
    row = lambda i: (i, 0)
    fix = lambda i: (0, 0)
    return pl.pallas_call(
        functools.partial(_mm_ln_kernel, alpha=None, residual=False, gelu=True, two_out=False),
        grid=(m // tm,),
        in_specs=[
            pl.BlockSpec((tm, k), row),
            _weight_spec(k, n, layer, colblk),
            pl.BlockSpec((1, n), fix),
            pl.BlockSpec((1, n), fix),
            pl.BlockSpec((1, n), fix),
        ],
        out_specs=pl.BlockSpec((tm, n), row),
        out_shape=jax.ShapeDtypeStruct((m, n), BF16),
        scratch_shapes=[pltpu.VMEM((k, n), BF16)],
        compiler_params=_params("arbitrary"),
        name="matmul_gelu_ln",
    )(h, w_all, bias.reshape(1, n), g.reshape(1, n), b.reshape(1, n))


def _gates_kernel(w_ref, x_ref, b_ref, o_ref, *, heads):
    g = lax.dot_general(w_ref[...].astype(BF16), x_ref[...], (((0,), (1,)), ((), ())),
                        preferred_element_type=F32)
    g = g + b_ref[...]
    r = lax.broadcasted_iota(jnp.int32, g.shape, 0)
    is_forget = ((r >= heads) & (r < 2 * heads)) | (r >= 3 * heads)
    log_sig = jnp.minimum(g, 0.0) - jnp.log1p(jnp.exp(-jnp.abs(g)))
    o_ref[...] = jnp.where(is_forget, log_sig, g)


def _mlstm_gates(x_bf, w_gates, b_gates, *, tm=1024):
    t, k = x_bf.shape
    ng = w_gates.shape[1]
    tm = min(tm, t)
    assert t % tm == 0
    return pl.pallas_call(
        functools.partial(_gates_kernel, heads=ng // 4),
        grid=(t // tm,),
        in_specs=[
            pl.BlockSpec((k, ng), lambda i: (0, 0)),
            pl.BlockSpec((tm, k), lambda i: (i, 0)),
            pl.BlockSpec((ng, 1), lambda i: (0, 0)),
        ],
        out_specs=pl.BlockSpec((ng, tm), lambda i: (0, i)),
        out_shape=jax.ShapeDtypeStruct((ng, t), F32),
        compiler_params=_params("parallel"),
        name="mlstm_gates",
    )(w_gates, x_bf, b_gates.reshape(ng, 1).astype(F32))


def _mlstm_kernel(q_ref, k_ref, v_ref, o_ref, gt_ref, ng_ref, out_ref, hfw_ref, c_ref, *, nc, scale):
    L = CHUNK
    dk = q_ref.shape[1]
    row_i = lax.broadcasted_iota(jnp.int32, (L, L), 0)
    col_i = lax.broadcasted_iota(jnp.int32, (L, L), 1)
    eye = row_i == col_i

    def chunk_step(c, carry, reverse):
        n, m = carry
        off = pl.multiple_of(c * L, L)
        qc = q_ref[pl.ds(off, L), :]
        kc = k_ref[pl.ds(off, L), :]
        vc = v_ref[pl.ds(off, L), :]
        rows = gt_ref[0, 0, c]
        r0 = 2 if reverse else 0
        i_row = rows[r0:r0 + 1, :]
        f_row = rows[r0 + 1:r0 + 2, :]
        mask = (col_i >= row_i) if reverse else (col_i <= row_i)
        b_col = jnp.sum(jnp.where(mask, f_row, 0.0), axis=1, keepdims=True)
        b_row = jnp.sum(jnp.where(eye, b_col, 0.0), axis=0, keepdims=True)
        i_col = jnp.sum(jnp.where(eye, i_row, 0.0), axis=1, keepdims=True)
        dmat = jnp.where(mask, b_col - b_row + i_row, -jnp.inf)
        inter = b_col + m
        m_t = jnp.maximum(inter, jnp.max(dmat, axis=1, keepdims=True))
        p = jnp.exp(dmat - m_t)
        sqk = lax.dot_general(qc, kc, (((1,), (1,)), ((), ())), preferred_element_type=F32)
        s = sqk * (scale * p)
        isc = jnp.exp(inter - m_t)
        qn = jnp.sum(qc.astype(F32) * n, axis=1, keepdims=True)
        den = jnp.sum(s, axis=1, keepdims=True) + isc * (scale * qn)
        c_old = c_ref[...]
        num = jnp.dot(s.astype(BF16), vc, preferred_element_type=F32)
        num = num + (isc * scale) * jnp.dot(qc, c_old.astype(BF16), preferred_element_type=F32)
        h = num / jnp.maximum(jnp.abs(den), jnp.exp(-m_t))
        b_last = jnp.sum(f_row, axis=1, keepdims=True)
        g_col = b_last - b_col + i_col
        m_new = jnp.maximum(b_last + m, jnp.max(g_col, axis=0, keepdims=True))
        decay = jnp.exp(b_last + m - m_new)
        wk = jnp.exp(g_col - m_new)
        kw = kc.astype(F32) * wk
        c_ref[...] = decay * c_old + lax.dot_general(
            kw.astype(BF16), vc, (((0,), (0,)), ((), ())), preferred_element_type=F32)
        n_new = decay * n + jnp.sum(kw, axis=0, keepdims=True)
        return h, off, (n_new, m_new)

    init = (jnp.zeros((1, dk), F32), jnp.zeros((1, 1), F32))

    def fw_body(c, carry):
        h, off, carry = chunk_step(c, carry, False)
        hfw_ref[pl.ds(off, L), :] = h
        return carry

    c_ref[...] = jnp.zeros_like(c_ref)
    lax.fori_loop(0, nc, fw_body, init)

    def bw_body(i, carry):
        h, off, carry = chunk_step(nc - 1 - i, carry, True)
        tot = hfw_ref[pl.ds(off, L), :] + h
        mu = jnp.mean(tot, axis=1, keepdims=True)
        hc = tot - mu
        var = jnp.mean(hc * hc, axis=1, keepdims=True)
        hn = hc * lax.rsqrt(var + LN_EPS) * ng_ref[...]
        og = jax.nn.sigmoid(o_ref[pl.ds(off, L), :].astype(F32))
        out_ref[pl.ds(off, L), :] = (og * hn).astype(out_ref.dtype)
        return carry

    c_ref[...] = jnp.zeros_like(c_ref)
    lax.fori_loop(0, nc, bw_body, init)


def _mlstm_scan(proj, gates_t, norm_g, *, batch, seq):
    H = MLSTM_HEADS
    L = CHUNK
    v_w = norm_g.shape[0]
    dv = v_w // H
    dk = (proj.shape[1] - 2 * v_w) // (2 * H)
    nc = seq // L
    assert seq % L == 0 and dv % LANES == 0 and dk % LANES == 0
    gt = gates_t.reshape(4, H, batch, nc, L).transpose(2, 1, 3, 0, 4)
    gt = jnp.pad(gt, ((0, 0), (0, 0), (0, 0), (0, 4), (0, 0)))
    k0 = H
    v0 = 2 * H * dk // dv
    o0 = v0 + H
    return pl.pallas_call(
        functools.partial(_mlstm_kernel, nc=nc, scale=float(dk) ** -0.5),
        grid=(batch, H),
        in_specs=[
            pl.BlockSpec((seq, dk), lambda b, h: (b, h)),
            pl.BlockSpec((seq, dk), lambda b, h: (b, k0 + h)),
            pl.BlockSpec((seq, dv), lambda b, h: (b, v0 + h)),
            pl.BlockSpec((seq, dv), lambda b, h: (b, o0 + h)),
            pl.BlockSpec((1, 1, nc, 8, L), lambda b, h: (b, h, 0, 0, 0)),
            pl.BlockSpec((1, dv), lambda b, h: (0, h)),
        ],
        out_specs=pl.BlockSpec((seq, dv), lambda b, h: (b, h)),
        out_shape=jax.ShapeDtypeStruct((batch * seq, v_w), BF16),
        scratch_shapes=[pltpu.VMEM((seq, dv), F32), pltpu.VMEM((dk, dv), F32)],
        compiler_params=_params("parallel", "parallel"),
        name="mlstm_scan",
    )(proj, proj, proj, proj, gt, norm_g.reshape(1, v_w).astype(F32))


def _spatial_kernel(ws_ref, bs_ref, v_ref, u_ref, y_ref, *, groups):
    gd = v_ref.shape[1] // groups
    for g in range(groups):
        cols = slice(g * gd, (g + 1) * gd)
        s = jnp.dot(ws_ref[g], v_ref[:, cols], preferred_element_type=F32) + bs_ref[g]
        y_ref[:, cols] = (u_ref[:, cols].astype(F32) * s).astype(y_ref.dtype)


def _gmlp_spatial(u, vn, w_s, b_s):
    t, width = u.shape
    groups, L, _ = w_s.shape
    blk = pl.BlockSpec((L, width), lambda c: (c, 0))
    return pl.pallas_call(
        functools.partial(_spatial_kernel, groups=groups),
        grid=(t // L,),
        in_specs=[
            pl.BlockSpec((groups, L, L), lambda c: (0, 0, 0)),
            pl.BlockSpec((groups, L, 1), lambda c: (0, 0, 0)),
            blk, blk,
        ],
        out_specs=blk,
        out_shape=jax.ShapeDtypeStruct((t, width), BF16),
        compiler_params=_params("parallel"),
        name="gmlp_spatial",
    )(w_s.astype(BF16), b_s.reshape(groups, L, 1).astype(F32), vn, u)


def _router_kernel(x_ref, w_ref, b_ref, gate_ref, idx_ref, *, n_experts):
    x = x_ref[...]
    w = w_ref[...]
    xh = x.astype(BF16)
    xl = (x - xh.astype(F32)).astype(BF16)
    wh = w.astype(BF16)
    wl = (w - wh.astype(F32)).astype(BF16)
    logits = (jnp.dot(xh, wh, preferred_element_type=F32) + jnp.dot(xh, wl, preferred_element_type=F32)
              + jnp.dot(xl, wh, preferred_element_type=F32)) + b_ref[...]
    lane = lax.broadcasted_iota(jnp.int32, logits.shape, 1)
    cur = jnp.where(lane < n_experts, logits, -jnp.inf)
    vals, idxs = [], []
    for _ in range(TOP_K):
        mx = jnp.max(cur, axis=1, keepdims=True)
        ix = jnp.min(jnp.where(cur == mx, lane, LANES), axis=1, keepdims=True)
        vals.append(mx)
        idxs.append(ix)
        cur = jnp.where(lane == ix, -jnp.inf, cur)
    exps = [jnp.exp(v - vals[0]) for v in vals]
    tot = exps[0]
    for e in exps[1:]:
        tot = tot + e
    gate = jnp.zeros(logits.shape, F32)
    idx = jnp.zeros(logits.shape, jnp.int32)
    for k in range(TOP_K):
        gate = jnp.where(lane == k, exps[k] / tot, gate)
        idx = jnp.where(lane == k, idxs[k], idx)
    gate_ref[...] = gate
    idx_ref[...] = idx


def _router(x, rw, rb, *, tm=512):
    t, d = x.shape
    e = rw.shape[1]
    assert e <= LANES
    tm = min(tm, t)
    w_pad = jnp.pad(rw.astype(F32), ((0, 0), (0, LANES - e)))
    b_pad = jnp.pad(rb.astype(F32), (0, LANES - e)).reshape(1, LANES)
    return pl.pallas_call(
        functools.partial(_router_kernel, n_experts=e),
        grid=(t // tm,),
        in_specs=[
            pl.BlockSpec((tm, d), lambda i: (i, 0)),
            pl.BlockSpec((d, LANES), lambda i: (0, 0)),
            pl.BlockSpec((1, LANES), lambda i: (0, 0)),
        ],
        out_specs=[pl.BlockSpec((tm, LANES), lambda i: (i, 0))] * 2,
        out_shape=[jax.ShapeDtypeStruct((t, LANES), F32), jax.ShapeDtypeStruct((t, LANES), jnp.int32)],
        compiler_params=_params("parallel"),
        name="moe_router",
    )(x, w_pad, b_pad)


RANK_BLOCK = 512


def _rank_kernel(e_ref, rank_ref, cnt_ref, carry_ref):
    @pl.when(pl.program_id(0) == 0)
    def _():
        carry_ref[...] = jnp.zeros_like(carry_ref)

    e_row = e_ref[0]
    ex = lax.broadcasted_iota(jnp.int32, (LANES, RANK_BLOCK), 0)
    onehot = e_row == ex
    a_i = lax.broadcasted_iota(jnp.int32, (RANK_BLOCK, RANK_BLOCK), 0)
    b_i = lax.broadcasted_iota(jnp.int32, (RANK_BLOCK, RANK_BLOCK), 1)
    before = (a_i < b_i).astype(BF16)
    within = jnp.dot(onehot.astype(BF16), before, preferred_element_type=F32)
    carry = carry_ref[...][:, :1]
    rank = jnp.sum(jnp.where(onehot, within + carry, 0.0), axis=0, keepdims=True)
    rank_ref[0] = rank.astype(jnp.int32)
    new = carry + jnp.sum(onehot.astype(F32), axis=1, keepdims=True)
    carry_ref[...] = jnp.broadcast_to(new, carry_ref.shape)
    cnt_ref[...] = jnp.broadcast_to(new, cnt_ref.shape).astype(jnp.int32)


def _expert_ranks(e_flat, n_experts):
    a = e_flat.shape[0]
    assert a % RANK_BLOCK == 0 and n_experts <= LANES
    nb = a // RANK_BLOCK
    rank, cnt = pl.pallas_call(
        _rank_kernel,
        grid=(nb,),
        in_specs=[pl.BlockSpec((1, 1, RANK_BLOCK), lambda i: (i, 0, 0))],
        out_specs=[pl.BlockSpec((1, 1, RANK_BLOCK), lambda i: (i, 0, 0)),
                   pl.BlockSpec((LANES, LANES), lambda i: (0, 0))],
        out_shape=[jax.ShapeDtypeStruct((nb, 1, RANK_BLOCK), jnp.int32),
                   jax.ShapeDtypeStruct((LANES, LANES), jnp.int32)],
        scratch_shapes=[pltpu.VMEM((LANES, LANES), F32)],
        compiler_params=_params("arbitrary"),
        name="moe_rank",
    )(e_flat.reshape(nb, 1, RANK_BLOCK))
    return rank.reshape(a), cnt[:n_experts, 0]


INVERT_CHUNK = 4096


def _invert_kernel(dest_ref, lo_ref, hi_ref, out_ref, *, chunk, n_exp, n_slots):
    step = pl.program_id(0)

    def fill_range(lo, hi):
        def fill(s, c):
            out_ref[s] = -1
            return c
        lax.fori_loop(lo, hi, fill, 0)

    @pl.when(step == 0)
    def _():
        for e in range(n_exp):
            fill_range(lo_ref[e], hi_ref[e])
        fill_range(hi_ref[n_exp - 1], n_slots)

    @pl.when(step > 0)
    def _():
        base = (step - 1) * chunk

        def put(i, c):
            a = base + i
            out_ref[dest_ref[a]] = a
            return c
        lax.fori_loop(0, chunk, put, 0, unroll=8)


def _invert_dest(dest, pad_lo, pad_hi, n_slots):
    a = dest.shape[0]
    chunk = min(INVERT_CHUNK, a)
    assert a % chunk == 0
    smem = pl.BlockSpec(memory_space=pltpu.SMEM)
    return pl.pallas_call(
        functools.partial(_invert_kernel, chunk=chunk, n_exp=pad_lo.shape[0], n_slots=n_slots),
        grid=(a // chunk + 1,),
        in_specs=[smem, smem, smem],
        out_specs=smem,
        out_shape=jax.ShapeDtypeStruct((n_slots,), jnp.int32),
        compiler_params=_params("arbitrary"),
        name="moe_invert",
    )(dest, pad_lo, pad_hi)


def _expert_kernel(be_ref, dst_ref, x_hbm, wgu_ref, bgu_ref, wdn_ref, bdn_ref, out_hbm,
                   xbuf0, xbuf1, ybuf0, ybuf1, hbuf, wgu_bf, wdn_bf, gsem, ssem, *, n_tokens, n_blocks, ff):
    j = pl.program_id(0)
    R = MOE_BLOCK
    xbufs = (xbuf0, xbuf1)
    ybufs = (ybuf0, ybuf1)

    @pl.when((j == 0) | (be_ref[j] != be_ref[jnp.maximum(j - 1, 0)]))
    def _():
        wgu_bf[...] = wgu_ref[...].astype(BF16)
        wdn_bf[...] = wdn_ref[...].astype(BF16)

    nch = xbuf0.shape[0] // R

    def slab_rows(ref, row):
        return ref.at[pl.ds(pl.multiple_of(row * nch, nch), nch)]

    def start_gather(blk, s):
        for r in range(R):
            tok = dst_ref[blk, r] & (n_tokens - 1)
            pltpu.make_async_copy(slab_rows(x_hbm, tok), slab_rows(xbufs[s], r), gsem.at[s]).start()

    def wait_gather(s):
        pltpu.make_async_copy(x_hbm.at[pl.ds(0, R * nch)], xbufs[s], gsem.at[s]).wait()

    def start_scatter(blk, s):
        for r in range(R):
            pltpu.make_async_copy(slab_rows(ybufs[s], r), slab_rows(out_hbm, dst_ref[blk, r]), ssem.at[s]).start()

    def wait_scatter(s):
        pltpu.make_async_copy(ybufs[s], out_hbm.at[pl.ds(0, R * nch)], ssem.at[s]).wait()

    def up_proj(s):
        xb = _from_slab(xbufs[s], R).astype(BF16)
        hgu = jnp.dot(xb, wgu_bf[...], preferred_element_type=F32) + bgu_ref[0]
        g = jnp.minimum(hgu[:, :ff], SWIGLU_LIMIT)
        u = jnp.clip(hgu[:, ff:], -SWIGLU_LIMIT, SWIGLU_LIMIT)
        hbuf[...] = ((u + 1.0) * (g * jax.nn.sigmoid(g * SWIGLU_ALPHA))).astype(BF16)

    def down_proj(s):
        _to_slab(ybufs[s], jnp.dot(hbuf[...], wdn_bf[...], preferred_element_type=F32) + bdn_ref[0])

    @pl.when(j == 0)
    def _():
        start_gather(0, 0)
        wait_gather(0)
        start_gather(1, 1)
        up_proj(0)
        down_proj(0)

    for s in (0, 1):
        @pl.when((j > 0) & (lax.rem(j, 2) == s))
        def _(s=s):
            wait_gather(s)
            start_gather(jnp.minimum(j + 1, n_blocks - 1), 1 - s)
            up_proj(s)

            @pl.when(j >= 2)
            def _():
                wait_scatter(s)

            start_scatter(j - 1, 1 - s)
            down_proj(s)

    @pl.when(j == n_blocks - 1)
    def _():
        s_last = (n_blocks - 1) % 2
        start_scatter(n_blocks - 1, s_last)
        wait_gather(1 - s_last)
        wait_scatter(1 - s_last)
        wait_scatter(s_last)


def _expert_blocks(x_slab, block_e, dst_rows, w_gu_all, w_dn_all, layer, b_gu, b_dn):
    d = w_gu_all.shape[2]
    nch = d // LANES
    t = x_slab.shape[0] // nch
    n_blocks = block_e.shape[0]
    _, n_exp, _, ff2 = w_gu_all.shape
    ff = ff2 // 2
    assert t & (t - 1) == 0, "token count must be a power of two"
    assert n_blocks >= 2
    wmap = lambda j, be, ds: (layer, be[j], 0, 0)
    bmap = lambda j, be, ds: (be[j], 0, 0)
    return pl.pallas_call(
        functools.partial(_expert_kernel, n_tokens=t, n_blocks=n_blocks, ff=ff),
        grid_spec=pltpu.PrefetchScalarGridSpec(
            num_scalar_prefetch=2,
            grid=(n_blocks,),
            in_specs=[
                pl.BlockSpec(memory_space=pl.ANY),
                pl.BlockSpec((None, None, d, ff2), wmap),
                pl.BlockSpec((1, 1, ff2), bmap),
                pl.BlockSpec((None, None, ff, d), wmap),
                pl.BlockSpec((1, 1, d), bmap),
            ],
            out_specs=pl.BlockSpec(memory_space=pl.ANY),
            scratch_shapes=[
                pltpu.VMEM((MOE_BLOCK * nch, LANES), F32),
                pltpu.VMEM((MOE_BLOCK * nch, LANES), F32),
                pltpu.VMEM((MOE_BLOCK * nch, LANES), F32),
                pltpu.VMEM((MOE_BLOCK * nch, LANES), F32),
                pltpu.VMEM((MOE_BLOCK, ff), BF16),
                pltpu.VMEM((d, ff2), BF16),
                pltpu.VMEM((ff, d), BF16),
                pltpu.SemaphoreType.DMA((2,)),
                pltpu.SemaphoreType.DMA((2,)),
            ],
        ),
        out_shape=jax.ShapeDtypeStruct((n_blocks * MOE_BLOCK * nch, LANES), F32),
        compiler_params=_params("arbitrary"),
        name="moe_experts",
    )(block_e, dst_rows, x_slab, w_gu_all, b_gu.reshape(n_exp, 1, ff2), w_dn_all, b_dn.reshape(n_exp, 1, d))


def _combine_kernel(x_ref, y0_ref, y1_ref, y2_ref, y3_ref, gate_ref, g_ref, b_ref, o_ref, obf_ref, *, alpha):
    gate = gate_ref[...]
    rows = x_ref.shape[0]
    ffn = gate[:, 0:1] * _from_slab(y0_ref, rows)
    for k, y_ref in enumerate((y1_ref, y2_ref, y3_ref), start=1):
        ffn = ffn + gate[:, k:k + 1] * _from_slab(y_ref, rows)
    y = _layer_norm(alpha * x_ref[...] + ffn, g_ref[...], b_ref[...])
    o_ref[...] = y
    obf_ref[...] = y.astype(BF16)


def _combine_ln(x, ys, gate_pad, g, b, alpha, *, tm=256):
    t, d = x.shape
    tm = min(tm, t)
    nt = t // tm
    row = lambda i: (i, 0)
    fix = lambda i: (0, 0)
    nch = d // LANES
    y_specs = [pl.BlockSpec((tm * nch, LANES), functools.partial(lambda i, k: (i + k * nt, 0), k=k))
               for k in range(TOP_K)]
    return pl.pallas_call(
        functools.partial(_combine_kernel, alpha=alpha),
        grid=(nt,),
        in_specs=[pl.BlockSpec((tm, d), row)] + y_specs + [
            pl.BlockSpec((tm, LANES), row),
            pl.BlockSpec((1, d), fix),
            pl.BlockSpec((1, d), fix),
        ],
        out_specs=[pl.BlockSpec((tm, d), row), pl.BlockSpec((tm, d), row)],
        out_shape=[jax.ShapeDtypeStruct((t, d), F32), jax.ShapeDtypeStruct((t, d), BF16)],
        compiler_params=_params("parallel"),
        name="moe_combine_ln",
    )(x, ys, ys, ys, ys, gate_pad, g.reshape(1, d), b.reshape(1, d))


def _moe_layer(x, x_slab, layer, rw, rb, w_gu_all, b_gu, w_dn_all, b_dn, g, b, alpha):
    t, d = x.shape
    n_exp = rw.shape[1]
    a = t * TOP_K
    gate_pad, idx_pad = _router(x, rw, rb)
    e_flat = idx_pad[:, :TOP_K].reshape(a)
    rank, counts = _expert_ranks(e_flat, n_exp)
    padded = (counts + MOE_BLOCK - 1) // MOE_BLOCK * MOE_BLOCK
    pend = jnp.cumsum(padded)
    pstart = pend - padded
    dest = pstart[e_flat] + rank
    n_slots = a + n_exp * MOE_BLOCK
    n_blocks = n_slots // MOE_BLOCK
    block_start = jnp.arange(n_blocks, dtype=jnp.int32) * MOE_BLOCK
    block_e = jnp.minimum(jnp.sum((block_start[:, None] >= pend[None, :]).astype(jnp.int32), axis=1), n_exp - 1)
    slot_a = _invert_dest(dest, pstart + counts, pend, n_slots)
    valid = slot_a >= 0
    pad_idx = jnp.cumsum(jnp.logical_not(valid).astype(jnp.int32)) - 1
    dst_rows = jnp.where(valid, (slot_a % TOP_K) * t + slot_a // TOP_K, a + pad_idx).astype(jnp.int32)
    ys = _expert_blocks(x_slab, block_e, dst_rows.reshape(n_blocks, MOE_BLOCK), w_gu_all, w_dn_all, layer, b_gu, b_dn)
    return _combine_ln(x, ys, gate_pad, g, b, alpha)


def _mlstm_layer(x, x_bf, w_in_all, j, b_gates, norm_g, w_out_all, g, b, alpha, *, batch, seq):
    H = MLSTM_HEADS
    n_main = w_in_all.shape[2] - 4 * H
    proj = _matmul(x_bf, w_in_all, j, 0, n_main, jnp.zeros((n_main,), F32))
    gates_t = _mlstm_gates(x_bf, w_in_all[j, :, n_main:], b_gates)
    h = _mlstm_scan(proj, gates_t, norm_g, batch=batch, seq=seq)
    return _matmul_res_ln(h, w_out_all, j, x, g, b, alpha)


def _gmlp_layer(x, x_bf, w_in_all, j, b_in, vn_g, vn_b, w_s, b_s, w_out_all, g, b, alpha):
    width = w_out_all.shape[1]
    u = _matmul(x_bf, w_in_all, j, 0, width, b_in[:width], act="gelu")
    vn = _matmul_gelu_ln(x_bf, w_in_all, j, 1, width, b_in[width:], vn_g, vn_b)
    y = _gmlp_spatial(u, vn, w_s, b_s)
    return _matmul_res_ln(y, w_out_all, j, x, g, b, alpha)


def kernel(x, mlstm_w_in, mlstm_b_gates, mlstm_norm_g, mlstm_w_out, gmlp_w_in, gmlp_b_in, gmlp_vnorm_g,
           gmlp_vnorm_b, gmlp_w_s, gmlp_b_s, gmlp_w_out, router_w, router_b, expert_w_gu, expert_b_gu,
           expert_w_down, expert_b_down, ln_g, ln_b):
    batch, seq, d = x.shape
    depth = ln_g.shape[0]
    alpha = float((2 * depth) ** 0.25)
    xf = x.reshape(batch * seq, d)
    x_bf = xf.astype(BF16)
    for layer in range(depth):
        j = layer // 2
        if layer % 2 == 0:
            xf, x_bf, x_slab = _mlstm_layer(xf, x_bf, mlstm_w_in, j, mlstm_b_gates[j], mlstm_norm_g[j], mlstm_w_out,
                                    ln_g[layer, 0], ln_b[layer, 0], alpha, batch=batch, seq=seq)
        else:
            xf, x_bf, x_slab = _gmlp_layer(xf, x_bf, gmlp_w_in, j, gmlp_b_in[j], gmlp_vnorm_g[j], gmlp_vnorm_b[j],
                                   gmlp_w_s[j], gmlp_b_s[j], gmlp_w_out,
                                   ln_g[layer, 0], ln_b[layer, 0], alpha)
        xf, x_bf = _moe_layer(xf, x_slab, layer, router_w[layer], router_b[layer],
                              expert_w_gu, expert_b_gu[layer], expert_w_down, expert_b_down[layer],
                              ln_g[layer, 1], ln_b[layer, 1], alpha)
    return xf.reshape(batch, seq, d)
```

```python
import functools

import jax
import jax.numpy as jnp
from jax import lax
from jax.experimental import pallas as pl
from jax.experimental.pallas import tpu as pltpu

MLSTM_HEADS = 8
CHUNK = 128
GMLP_GROUPS = 8
TOP_K = 4
MOE_BLOCK = 128
SWIGLU_LIMIT = 7.0
SWIGLU_ALPHA = 1.702
LN_EPS = 1e-5
LANES = 128
VMEM_LIMIT = 56 * 1024 * 1024

F32 = jnp.float32
BF16 = jnp.bfloat16


def _params(*sem):
    return pltpu.CompilerParams(dimension_semantics=sem, vmem_limit_bytes=VMEM_LIMIT)


def _gelu(x):
    return 0.5 * x * (1.0 + lax.erf(x * (2.0 ** -0.5)))


def _layer_norm(z, g, b):
    mu = jnp.mean(z, axis=-1, keepdims=True)
    zc = z - mu
    var = jnp.mean(zc * zc, axis=-1, keepdims=True)
    return zc * lax.rsqrt(var + LN_EPS) * g + b


def _mm_kernel(x_ref, w_ref, b_ref, o_ref, wbf_ref, *, act):
    @pl.when(pl.program_id(1) == 0)
    def _():
        wbf_ref[...] = w_ref[...].astype(BF16)

    acc = jnp.dot(x_ref[...], wbf_ref[...], preferred_element_type=F32)
    acc = acc + b_ref[...]
    if act == "gelu":
        acc = _gelu(acc)
    o_ref[...] = acc.astype(o_ref.dtype)


def _matmul(x, w_all, layer, col0, n, b, *, act=None, out_dtype=BF16, tm=1024, tn=1024):
    m, k = x.shape
    tm, tn = min(tm, m), min(tn, n)
    assert m % tm == 0 and n % tn == 0 and col0 % tn == 0
    c0 = col0 // tn
    return pl.pallas_call(
        functools.partial(_mm_kernel, act=act),
        grid=(n // tn, m // tm),
        in_specs=[
            pl.BlockSpec((tm, k), lambda j, i: (i, 0)),
            pl.BlockSpec((None, k, tn), lambda j, i: (layer, 0, c0 + j)),
            pl.BlockSpec((1, tn), lambda j, i: (0, j)),
        ],
        out_specs=pl.BlockSpec((tm, tn), lambda j, i: (i, j)),
        out_shape=jax.ShapeDtypeStruct((m, n), out_dtype),
        scratch_shapes=[pltpu.VMEM((k, tn), BF16)],
        compiler_params=_params("parallel", "arbitrary"),
        name="matmul_bias_act",
    )(x, w_all, b.reshape(1, n).astype(F32))


def _mm_ln_kernel(*refs, alpha, residual, gelu, two_out):
    wbf_ref = refs[-1]
    refs = refs[:-1]
    if residual:
        h_ref, w_ref, res_ref, g_ref, b_ref = refs[:5]
        outs = refs[5:]
    else:
        h_ref, w_ref, bias_ref, g_ref, b_ref = refs[:5]
        outs = refs[5:]

    @pl.when(pl.program_id(0) == 0)
    def _():
        wbf_ref[...] = w_ref[...].astype(BF16)

    acc = jnp.dot(h_ref[...], wbf_ref[...], preferred_element_type=F32)
    if residual:
        z = alpha * res_ref[...] + acc
    else:
        z = acc + bias_ref[...]
    if gelu:
        z = _gelu(z)
    y = _layer_norm(z, g_ref[...], b_ref[...])
    if two_out:
        outs[0][...] = y
        outs[1][...] = y.astype(BF16)
    else:
        outs[0][...] = y.astype(outs[0].dtype)


def _weight_spec(k, n, layer, colblk):
    return pl.BlockSpec((None, k, n), lambda i: (layer, 0, colblk), pipeline_mode=pl.Buffered(1))


def _matmul_res_ln(h, w_all, layer, res, g, b, alpha, *, tm=256):
    m, k = h.shape
    n = w_all.shape[2]
    tm = min(tm, m)
    assert m % tm == 0
    row = lambda i: (i, 0)
    fix = lambda i: (0, 0)
    return pl.pallas_call(
        functools.partial(_mm_ln_kernel, alpha=alpha, residual=True, gelu=False, two_out=True),
        grid=(m // tm,),
        in_specs=[
            pl.BlockSpec((tm, k), row),
            _weight_spec(k, n, layer, 0),
            pl.BlockSpec((tm, n), row),
            pl.BlockSpec((1, n), fix),
            pl.BlockSpec((1, n), fix),
        ],
        out_specs=[pl.BlockSpec((tm, n), row), pl.BlockSpec((tm, n), row)],
        out_shape=[jax.ShapeDtypeStruct((m, n), F32), jax.ShapeDtypeStruct((m, n), BF16)],
        scratch_shapes=[pltpu.VMEM((k, n), BF16)],
        compiler_params=_params("arbitrary"),
        name="matmul_residual_ln",
    )(h, w_all, res, g.reshape(1, n), b.reshape(1, n))


def _matmul_gelu_ln(h, w_all, layer, colblk, n, bias, g, b, *, tm=256):
    m, k = h.shape
    tm = min(tm, m)
    assert m % tm == 0
    row = lambda i: (i, 0)
    fix = lambda i: (0, 0)
    return pl.pallas_call(
        functools.partial(_mm_ln_kernel, alpha=None, residual=False, gelu=True, two_out=False),
        grid=(m // tm,),
        in_specs=[
            pl.BlockSpec((tm, k), row),
            _weight_spec(k, n, layer, colblk),
            pl.BlockSpec((1, n), fix),
            pl.BlockSpec((1, n), fix),
            pl.BlockSpec((1, n), fix),
        ],
        out_specs=pl.BlockSpec((tm, n), row),
        out_shape=jax.ShapeDtypeStruct((m, n), BF16),
        scratch_shapes=[pltpu.VMEM((k, n), BF16)],
        compiler_params=_params("arbitrary"),
        name="matmul_gelu_ln",
    )(h, w_all, bias.reshape(1, n), g.reshape(1, n), b.reshape(1, n))


def _gates_kernel(w_ref, x_ref, b_ref, o_ref, *, heads):
    g = lax.dot_general(w_ref[...].astype(BF16), x_ref[...], (((0,), (1,)), ((), ())),
                        preferred_element_type=F32)
    g = g + b_ref[...]
    r = lax.broadcasted_iota(jnp.int32, g.shape, 0)
    is_forget = ((r >= heads) & (r < 2 * heads)) | (r >= 3 * heads)
    log_sig = jnp.minimum(g, 0.0) - jnp.log1p(jnp.exp(-jnp.abs(g)))
    o_ref[...] = jnp.where(is_forget, log_sig, g)


def _mlstm_gates(x_bf, w_gates, b_gates, *, tm=1024):
    t, k = x_bf.shape
    ng = w_gates.shape[1]
    tm = min(tm, t)
    assert t % tm == 0
    return pl.pallas_call(
        functools.partial(_gates_kernel, heads=ng // 4),
        grid=(t // tm,),
        in_specs=[
            pl.BlockSpec((k, ng), lambda i: (0, 0)),
            pl.BlockSpec((tm, k), lambda i: (i, 0)),
            pl.BlockSpec((ng, 1), lambda i: (0, 0)),
        ],
        out_specs=pl.BlockSpec((ng, tm), lambda i: (0, i)),
        out_shape=jax.ShapeDtypeStruct((ng, t), F32),
        compiler_params=_params("parallel"),
        name="mlstm_gates",
    )(w_gates, x_bf, b_gates.reshape(ng, 1).astype(F32))


def _mlstm_kernel(q_ref, k_ref, v_ref, o_ref, gt_ref, ng_ref, out_ref, hfw_ref, hbw_ref, cfw_ref, cbw_ref,
                  *, nc, scale):
    L = CHUNK
    dk = q_ref.shape[1]
    row_i = lax.broadcasted_iota(jnp.int32, (L, L), 0)
    col_i = lax.broadcasted_iota(jnp.int32, (L, L), 1)
    eye = row_i == col_i

    def chunk_step(c, carry, c_ref, reverse):
        n, m = carry
        off = pl.multiple_of(c * L, L)
        qc = q_ref[pl.ds(off, L), :]
        kc = k_ref[pl.ds(off, L), :]
        vc = v_ref[pl.ds(off, L), :]
        rows = gt_ref[0, 0, c]
        r0 = 2 if reverse else 0
        i_row = rows[r0:r0 + 1, :]
        f_row = rows[r0 + 1:r0 + 2, :]
        mask = (col_i >= row_i) if reverse else (col_i <= row_i)
        b_col = jnp.sum(jnp.where(mask, f_row, 0.0), axis=1, keepdims=True)
        b_row = jnp.sum(jnp.where(eye, b_col, 0.0), axis=0, keepdims=True)
        i_col = jnp.sum(jnp.where(eye, i_row, 0.0), axis=1, keepdims=True)
        dmat = jnp.where(mask, b_col - b_row + i_row, -jnp.inf)
        inter = b_col + m
        m_t = jnp.maximum(inter, jnp.max(dmat, axis=1, keepdims=True))
        p = jnp.exp(dmat - m_t)
        sqk = lax.dot_general(qc, kc, (((1,), (1,)), ((), ())), preferred_element_type=F32)
        s = sqk * (scale * p)
        isc = jnp.exp(inter - m_t)
        qn = jnp.sum(qc.astype(F32) * n, axis=1, keepdims=True)
        den = jnp.sum(s, axis=1, keepdims=True) + isc * (scale * qn)
        c_old = c_ref[...]
        num = jnp.dot(s.astype(BF16), vc, preferred_element_type=F32)
        num = num + (isc * scale) * jnp.dot(qc, c_old.astype(BF16), preferred_element_type=F32)
        h = num / jnp.maximum(jnp.abs(den), jnp.exp(-m_t))
        b_last = jnp.sum(f_row, axis=1, keepdims=True)
        g_col = b_last - b_col + i_col
        m_new = jnp.maximum(b_last + m, jnp.max(g_col, axis=0, keepdims=True))
        decay = jnp.exp(b_last + m - m_new)
        wk = jnp.exp(g_col - m_new)
        kw = kc.astype(F32) * wk
        c_ref[...] = decay * c_old + lax.dot_general(
            kw.astype(BF16), vc, (((0,), (0,)), ((), ())), preferred_element_type=F32)
        n_new = decay * n + jnp.sum(kw, axis=0, keepdims=True)
        return h, off, (n_new, m_new)

    init = (jnp.zeros((1, dk), F32), jnp.zeros((1, 1), F32))

    def scan_body(i, carry):
        fw, bw = carry
        h, off, fw = chunk_step(i, fw, cfw_ref, False)
        hfw_ref[pl.ds(off, L), :] = h
        h, off, bw = chunk_step(nc - 1 - i, bw, cbw_ref, True)
        hbw_ref[pl.ds(off, L), :] = h
        return fw, bw

    cfw_ref[...] = jnp.zeros_like(cfw_ref)
    cbw_ref[...] = jnp.zeros_like(cbw_ref)
    lax.fori_loop(0, nc, scan_body, (init, init), unroll=2)

    def finish_body(c, carry):
        off = pl.multiple_of(c * L, L)
        tot = hfw_ref[pl.ds(off, L), :] + hbw_ref[pl.ds(off, L), :]
        mu = jnp.mean(tot, axis=1, keepdims=True)
        hc = tot - mu
        var = jnp.mean(hc * hc, axis=1, keepdims=True)
        hn = hc * lax.rsqrt(var + LN_EPS) * ng_ref[...]
        og = jax.nn.sigmoid(o_ref[pl.ds(off, L), :].astype(F32))
        out_ref[pl.ds(off, L), :] = (og * hn).astype(out_ref.dtype)
        return carry

    lax.fori_loop(0, nc, finish_body, 0, unroll=4)


def _mlstm_scan(proj, gates_t, norm_g, *, batch, seq):
    H = MLSTM_HEADS
    L = CHUNK
    v_w = norm_g.shape[0]
    dv = v_w // H
    dk = (proj.shape[1] - 2 * v_w) // (2 * H)
    nc = seq // L
    assert seq % L == 0 and dv % LANES == 0 and dk % LANES == 0
    gt = gates_t.reshape(4, H, batch, nc, L).transpose(2, 1, 3, 0, 4)
    gt = jnp.pad(gt, ((0, 0), (0, 0), (0, 0), (0, 4), (0, 0)))
    k0 = H
    v0 = 2 * H * dk // dv
    o0 = v0 + H
    return pl.pallas_call(
        functools.partial(_mlstm_kernel, nc=nc, scale=float(dk) ** -0.5),
        grid=(batch, H),
        in_specs=[
            pl.BlockSpec((seq, dk), lambda b, h: (b, h)),
            pl.BlockSpec((seq, dk), lambda b, h: (b, k0 + h)),
            pl.BlockSpec((seq, dv), lambda b, h: (b, v0 + h)),
            pl.BlockSpec((seq, dv), lambda b, h: (b, o0 + h)),
            pl.BlockSpec((1, 1, nc, 8, L), lambda b, h: (b, h, 0, 0, 0)),
            pl.BlockSpec((1, dv), lambda b, h: (0, h)),
        ],
        out_specs=pl.BlockSpec((seq, dv), lambda b, h: (b, h)),
        out_shape=jax.ShapeDtypeStruct((batch * seq, v_w), BF16),
        scratch_shapes=[pltpu.VMEM((seq, dv), F32), pltpu.VMEM((seq, dv), F32),
                        pltpu.VMEM((dk, dv), F32), pltpu.VMEM((dk, dv), F32)],
        compiler_params=_params("parallel", "parallel"),
        name="mlstm_scan",
    )(proj, proj, proj, proj, gt, norm_g.reshape(1, v_w).astype(F32))


def _spatial_kernel(ws_ref, bs_ref, v_ref, u_ref, y_ref, *, groups):
    gd = v_ref.shape[1] // groups
    for g in range(groups):
        cols = slice(g * gd, (g + 1) * gd)
        s = jnp.dot(ws_ref[g], v_ref[:, cols], preferred_element_type=F32) + bs_ref[g]
        y_ref[:, cols] = (u_ref[:, cols].astype(F32) * s).astype(y_ref.dtype)


def _gmlp_spatial(u, vn, w_s, b_s):
    t, width = u.shape
    groups, L, _ = w_s.shape
    blk = pl.BlockSpec((L, width), lambda c: (c, 0))
    return pl.pallas_call(
        functools.partial(_spatial_kernel, groups=groups),
        grid=(t // L,),
        in_specs=[
            pl.BlockSpec((groups, L, L), lambda c: (0, 0, 0)),
            pl.BlockSpec((groups, L, 1), lambda c: (0, 0, 0)),
            blk, blk,
        ],
        out_specs=blk,
        out_shape=jax.ShapeDtypeStruct((t, width), BF16),
        compiler_params=_params("parallel"),
        name="gmlp_spatial",
    )(w_s.astype(BF16), b_s.reshape(groups, L, 1).astype(F32), vn, u)


def _router_kernel(x_ref, w_ref, b_ref, gate_ref, idx_ref, *, n_experts):
    x = x_ref[...]
    w = w_ref[...]
    xh = x.astype(BF16)
    xl = (x - xh.astype(F32)).astype(BF16)
    wh = w.astype(BF16)
    wl = (w - wh.astype(F32)).astype(BF16)
    logits = (jnp.dot(xh, wh, preferred_element_type=F32) + jnp.dot(xh, wl, preferred_element_type=F32)
              + jnp.dot(xl, wh, preferred_element_type=F32)) + b_ref[...]
    lane = lax.broadcasted_iota(jnp.int32, logits.shape, 1)
    cur = jnp.where(lane < n_experts, logits, -jnp.inf)
    vals, idxs = [], []
    for _ in range(TOP_K):
        mx = jnp.max(cur, axis=1, keepdims=True)
        ix = jnp.min(jnp.where(cur == mx, lane, LANES), axis=1, keepdims=True)
        vals.append(mx)
        idxs.append(ix)
        cur = jnp.where(lane == ix, -jnp.inf, cur)
    exps = [jnp.exp(v - vals[0]) for v in vals]
    tot = exps[0]
    for e in exps[1:]:
        tot = tot + e
    gate = jnp.zeros(logits.shape, F32)
    idx = jnp.zeros(logits.shape, jnp.int32)
    for k in range(TOP_K):
        gate = jnp.where(lane == k, exps[k] / tot, gate)
        idx = jnp.where(lane == k, idxs[k], idx)
    gate_ref[...] = gate
    idx_ref[...] = idx


def _router(x, rw, rb, *, tm=512):
    t, d = x.shape
    e = rw.shape[1]
    assert e <= LANES
    tm = min(tm, t)
    w_pad = jnp.pad(rw.astype(F32), ((0, 0), (0, LANES - e)))
    b_pad = jnp.pad(rb.astype(F32), (0, LANES - e)).reshape(1, LANES)
    return pl.pallas_call(
        functools.partial(_router_kernel, n_experts=e),
        grid=(t // tm,),
        in_specs=[
            pl.BlockSpec((tm, d), lambda i: (i, 0)),
            pl.BlockSpec((d, LANES), lambda i: (0, 0)),
            pl.BlockSpec((1, LANES), lambda i: (0, 0)),
        ],
        out_specs=[pl.BlockSpec((tm, LANES), lambda i: (i, 0))] * 2,
        out_shape=[jax.ShapeDtypeStruct((t, LANES), F32), jax.ShapeDtypeStruct((t, LANES), jnp.int32)],
        compiler_params=_params("parallel"),
        name="moe_router",
    )(x, w_pad, b_pad)


RANK_BLOCK = 512


def _rank_kernel(e_ref, rank_ref, cnt_ref, carry_ref):
    @pl.when(pl.program_id(0) == 0)
    def _():
        carry_ref[...] = jnp.zeros_like(carry_ref)

    e_row = e_ref[0]
    ex = lax.broadcasted_iota(jnp.int32, (LANES, RANK_BLOCK), 0)
    onehot = e_row == ex
    a_i = lax.broadcasted_iota(jnp.int32, (RANK_BLOCK, RANK_BLOCK), 0)
    b_i = lax.broadcasted_iota(jnp.int32, (RANK_BLOCK, RANK_BLOCK), 1)
    before = (a_i < b_i).astype(BF16)
    within = jnp.dot(onehot.astype(BF16), before, preferred_element_type=F32)
    carry = carry_ref[...][:, :1]
    rank = jnp.sum(jnp.where(onehot, within + carry, 0.0), axis=0, keepdims=True)
    rank_ref[0] = rank.astype(jnp.int32)
    new = carry + jnp.sum(onehot.astype(F32), axis=1, keepdims=True)
    carry_ref[...] = jnp.broadcast_to(new, carry_ref.shape)
    cnt_ref[...] = jnp.broadcast_to(new, cnt_ref.shape).astype(jnp.int32)


def _expert_ranks(e_flat, n_experts):
    a = e_flat.shape[0]
    assert a % RANK_BLOCK == 0 and n_experts <= LANES
    nb = a // RANK_BLOCK
    rank, cnt = pl.pallas_call(
        _rank_kernel,
        grid=(nb,),
        in_specs=[pl.BlockSpec((1, 1, RANK_BLOCK), lambda i: (i, 0, 0))],
        out_specs=[pl.BlockSpec((1, 1, RANK_BLOCK), lambda i: (i, 0, 0)),
                   pl.BlockSpec((LANES, LANES), lambda i: (0, 0))],
        out_shape=[jax.ShapeDtypeStruct((nb, 1, RANK_BLOCK), jnp.int32),
                   jax.ShapeDtypeStruct((LANES, LANES), jnp.int32)],
        scratch_shapes=[pltpu.VMEM((LANES, LANES), F32)],
        compiler_params=_params("arbitrary"),
        name="moe_rank",
    )(e_flat.reshape(nb, 1, RANK_BLOCK))
    return rank.reshape(a), cnt[:n_experts, 0]


INVERT_CHUNK = 4096


def _invert_kernel(dest_ref, lo_ref, hi_ref, out_ref, *, chunk, n_exp, n_slots):
    step = pl.program_id(0)

    def fill_range(lo, hi):
        def fill(s, c):
            out_ref[s] = -1
            return c
        lax.fori_loop(lo, hi, fill, 0)

    @pl.when(step == 0)
    def _():
        for e in range(n_exp):
            fill_range(lo_ref[e], hi_ref[e])
        fill_range(hi_ref[n_exp - 1], n_slots)

    @pl.when(step > 0)
    def _():
        base = (step - 1) * chunk

        def put(i, c):
            a = base + i
            out_ref[dest_ref[a]] = a
            return c
        lax.fori_loop(0, chunk, put, 0, unroll=8)


def _invert_dest(dest, pad_lo, pad_hi, n_slots):
    a = dest.shape[0]
    chunk = min(INVERT_CHUNK, a)
    assert a % chunk == 0
    smem = pl.BlockSpec(memory_space=pltpu.SMEM)
    return pl.pallas_call(
        functools.partial(_invert_kernel, chunk=chunk, n_exp=pad_lo.shape[0], n_slots=n_slots),
        grid=(a // chunk + 1,),
        in_specs=[smem, smem, smem],
        out_specs=smem,
        out_shape=jax.ShapeDtypeStruct((n_slots,), jnp.int32),
        compiler_params=_params("arbitrary"),
        name="moe_invert",
    )(dest, pad_lo, pad_hi)


def _expert_kernel(be_ref, dst_ref, x_hbm, wgu_ref, bgu_ref, wdn_ref, bdn_ref, out_hbm,
                   xbuf0, xbuf1, ybuf0, ybuf1, hbuf, wgu_bf, wdn_bf, gsem, ssem, *, n_tokens, n_blocks, ff):
    j = pl.program_id(0)
    R = MOE_BLOCK
    xbufs = (xbuf0, xbuf1)
    ybufs = (ybuf0, ybuf1)

    @pl.when((j == 0) | (be_ref[j] != be_ref[jnp.maximum(j - 1, 0)]))
    def _():
        wgu_bf[...] = wgu_ref[...].astype(BF16)
        wdn_bf[...] = wdn_ref[...].astype(BF16)

    def start_gather(blk, s):
        for r in range(R):
            tok = dst_ref[blk, r] & (n_tokens - 1)
            pltpu.make_async_copy(x_hbm.at[pl.ds(tok, 1)], xbufs[s].at[pl.ds(r, 1)], gsem.at[s]).start()

    def wait_gather(s):
        pltpu.make_async_copy(x_hbm.at[pl.ds(0, R)], xbufs[s], gsem.at[s]).wait()

    def start_scatter(blk, s):
        for r in range(R):
            pltpu.make_async_copy(ybufs[s].at[pl.ds(r, 1)], out_hbm.at[pl.ds(dst_ref[blk, r], 1)],
                                  ssem.at[s]).start()

    def wait_scatter(s):
        pltpu.make_async_copy(ybufs[s], out_hbm.at[pl.ds(0, R)], ssem.at[s]).wait()

    def up_proj(s):
        xb = xbufs[s][...].astype(BF16)
        hgu = jnp.dot(xb, wgu_bf[...], preferred_element_type=F32) + bgu_ref[0]
        g = jnp.minimum(hgu[:, :ff], SWIGLU_LIMIT)
        u = jnp.clip(hgu[:, ff:], -SWIGLU_LIMIT, SWIGLU_LIMIT)
        hbuf[...] = ((u + 1.0) * (g * jax.nn.sigmoid(g * SWIGLU_ALPHA))).astype(BF16)

    def down_proj(s):
        ybufs[s][...] = jnp.dot(hbuf[...], wdn_bf[...], preferred_element_type=F32) + bdn_ref[0]

    @pl.when(j == 0)
    def _():
        start_gather(0, 0)
        wait_gather(0)
        start_gather(1, 1)
        up_proj(0)
        down_proj(0)

    for s in (0, 1):
        @pl.when((j > 0) & (lax.rem(j, 2) == s))
        def _(s=s):
            wait_gather(s)
            start_gather(jnp.minimum(j + 1, n_blocks - 1), 1 - s)
            up_proj(s)

            @pl.when(j >= 2)
            def _():
                wait_scatter(s)

            start_scatter(j - 1, 1 - s)
            down_proj(s)

    @pl.when(j == n_blocks - 1)
    def _():
        s_last = (n_blocks - 1) % 2
        start_scatter(n_blocks - 1, s_last)
        wait_gather(1 - s_last)
        wait_scatter(1 - s_last)
        wait_scatter(s_last)


def _expert_blocks(x, block_e, dst_rows, w_gu_all, w_dn_all, layer, b_gu, b_dn):
    t, d = x.shape
    n_blocks = block_e.shape[0]
    _, n_exp, _, ff2 = w_gu_all.shape
    ff = ff2 // 2
    assert t & (t - 1) == 0, "token count must be a power of two"
    assert n_blocks >= 2
    wmap = lambda j, be, ds: (layer, be[j], 0, 0)
    bmap = lambda j, be, ds: (be[j], 0, 0)
    return pl.pallas_call(
        functools.partial(_expert_kernel, n_tokens=t, n_blocks=n_blocks, ff=ff),
        grid_spec=pltpu.PrefetchScalarGridSpec(
            num_scalar_prefetch=2,
            grid=(n_blocks,),
            in_specs=[
                pl.BlockSpec(memory_space=pl.ANY),
                pl.BlockSpec((None, None, d, ff2), wmap),
                pl.BlockSpec((1, 1, ff2), bmap),
                pl.BlockSpec((None, None, ff, d), wmap),
                pl.BlockSpec((1, 1, d), bmap),
            ],
            out_specs=pl.BlockSpec(memory_space=pl.ANY),
            scratch_shapes=[
                pltpu.VMEM((MOE_BLOCK, d), F32),
                pltpu.VMEM((MOE_BLOCK, d), F32),
                pltpu.VMEM((MOE_BLOCK, d), F32),
                pltpu.VMEM((MOE_BLOCK, d), F32),
                pltpu.VMEM((MOE_BLOCK, ff), BF16),
                pltpu.VMEM((d, ff2), BF16),
                pltpu.VMEM((ff, d), BF16),
                pltpu.SemaphoreType.DMA((2,)),
                pltpu.SemaphoreType.DMA((2,)),
            ],
        ),
        out_shape=jax.ShapeDtypeStruct((n_blocks * MOE_BLOCK, d), F32),
        compiler_params=_params("arbitrary"),
        name="moe_experts",
    )(block_e, dst_rows, x, w_gu_all, b_gu.reshape(n_exp, 1, ff2), w_dn_all, b_dn.reshape(n_exp, 1, d))


def _combine_kernel(x_ref, y0_ref, y1_ref, y2_ref, y3_ref, gate_ref, g_ref, b_ref, o_ref, obf_ref, *, alpha):
    gate = gate_ref[...]
    ffn = gate[:, 0:1] * y0_ref[...]
    for k, y_ref in enumerate((y1_ref, y2_ref, y3_ref), start=1):
        ffn = ffn + gate[:, k:k + 1] * y_ref[...]
    y = _layer_norm(alpha * x_ref[...] + ffn, g_ref[...], b_ref[...])
    o_ref[...] = y
    obf_ref[...] = y.astype(BF16)


def _combine_ln(x, ys, gate_pad, g, b, alpha, *, tm=256):
    t, d = x.shape
    tm = min(tm, t)
    nt = t // tm
    row = lambda i: (i, 0)
    fix = lambda i: (0, 0)
    y_specs = [pl.BlockSpec((tm, d), functools.partial(lambda i, k: (i + k * nt, 0), k=k)) for k in range(TOP_K)]
    return pl.pallas_call(
        functools.partial(_combine_kernel, alpha=alpha),
        grid=(nt,),
        in_specs=[pl.BlockSpec((tm, d), row)] + y_specs + [
            pl.BlockSpec((tm, LANES), row),
            pl.BlockSpec((1, d), fix),
            pl.BlockSpec((1, d), fix),
        ],
        out_specs=[pl.BlockSpec((tm, d), row), pl.BlockSpec((tm, d), row)],
        out_shape=[jax.ShapeDtypeStruct((t, d), F32), jax.ShapeDtypeStruct((t, d), BF16)],
        compiler_params=_params("parallel"),
        name="moe_combine_ln",
    )(x, ys, ys, ys, ys, gate_pad, g.reshape(1, d), b.reshape(1, d))


def _moe_layer(x, layer, rw, rb, w_gu_all, b_gu, w_dn_all, b_dn, g, b, alpha):
    t, d = x.shape
    n_exp = rw.shape[1]
    a = t * TOP_K
    gate_pad, idx_pad = _router(x, rw, rb)
    e_flat = idx_pad[:, :TOP_K].reshape(a)
    rank, counts = _expert_ranks(e_flat, n_exp)
    padded = (counts + MOE_BLOCK - 1) // MOE_BLOCK * MOE_BLOCK
    pend = jnp.cumsum(padded)
    pstart = pend - padded
    dest = pstart[e_flat] + rank
    n_slots = a + n_exp * MOE_BLOCK
    n_blocks = n_slots // MOE_BLOCK
    block_start = jnp.arange(n_blocks, dtype=jnp.int32) * MOE_BLOCK
    block_e = jnp.minimum(jnp.sum((block_start[:, None] >= pend[None, :]).astype(jnp.int32), axis=1), n_exp - 1)
    slot_a = _invert_dest(dest, pstart + counts, pend, n_slots)
    valid = slot_a >= 0
    pad_idx = jnp.cumsum(jnp.logical_not(valid).astype(jnp.int32)) - 1
    dst_rows = jnp.where(valid, (slot_a % TOP_K) * t + slot_a // TOP_K, a + pad_idx).astype(jnp.int32)
    ys = _expert_blocks(x, block_e, dst_rows.reshape(n_blocks, MOE_BLOCK), w_gu_all, w_dn_all, layer, b_gu, b_dn)
    return _combine_ln(x, ys, gate_pad, g, b, alpha)


def _mlstm_layer(x, x_bf, w_in_all, j, b_gates, norm_g, w_out_all, g, b, alpha, *, batch, seq):
    H = MLSTM_HEADS
    n_main = w_in_all.shape[2] - 4 * H
    proj = _matmul(x_bf, w_in_all, j, 0, n_main, jnp.zeros((n_main,), F32))
    gates_t = _mlstm_gates(x_bf, w_in_all[j, :, n_main:], b_gates)
    h = _mlstm_scan(proj, gates_t, norm_g, batch=batch, seq=seq)
    return _matmul_res_ln(h, w_out_all, j, x, g, b, alpha)


def _gmlp_layer(x, x_bf, w_in_all, j, b_in, vn_g, vn_b, w_s, b_s, w_out_all, g, b, alpha):
    width = w_out_all.shape[1]
    u = _matmul(x_bf, w_in_all, j, 0, width, b_in[:width], act="gelu")
    vn = _matmul_gelu_ln(x_bf, w_in_all, j, 1, width, b_in[width:], vn_g, vn_b)
    y = _gmlp_spatial(u, vn, w_s, b_s)
    return _matmul_res_ln(y, w_out_all, j, x, g, b, alpha)


def kernel(x, mlstm_w_in, mlstm_b_gates, mlstm_norm_g, mlstm_w_out, gmlp_w_in, gmlp_b_in, gmlp_vnorm_g,
           gmlp_vnorm_b, gmlp_w_s, gmlp_b_s, gmlp_w_out, router_w, router_b, expert_w_gu, expert_b_gu,
           expert_w_down, expert_b_down, ln_g, ln_b):
    batch, seq, d = x.shape
    depth = ln_g.shape[0]
    alpha = float((2 * depth) ** 0.25)
    xf = x.reshape(batch * seq, d)
    x_bf = xf.astype(BF16)
    for layer in range(depth):
        j = layer // 2
        if layer % 2 == 0:
            xf, x_bf = _mlstm_layer(xf, x_bf, mlstm_w_in, j, mlstm_b_gates[j], mlstm_norm_g[j], mlstm_w_out,
                                    ln_g[layer, 0], ln_b[layer, 0], alpha, batch=batch, seq=seq)
        else:
            xf, x_bf = _gmlp_layer(xf, x_bf, gmlp_w_in, j, gmlp_b_in[j], gmlp_vnorm_g[j], gmlp_vnorm_b[j],
                                   gmlp_w_s[j], gmlp_b_s[j], gmlp_w_out,
                                   ln_g[layer, 0], ln_b[layer, 0], alpha)
        xf, x_bf = _moe_layer(xf, layer, router_w[layer], router_b[layer],
                              expert_w_gu, expert_b_gu[layer], expert_w_down, expert_b_down[layer],
                              ln_g[layer, 1], ln_b[layer, 1], alpha)
    return xf.reshape(batch, seq, d)
```

```python
import functools

import jax
import jax.numpy as jnp
from jax import lax
from jax.experimental import pallas as pl
from jax.experimental.pallas import tpu as pltpu

MLSTM_HEADS = 8
CHUNK = 128
GMLP_GROUPS = 8
TOP_K = 4
MOE_BLOCK = 128
SWIGLU_LIMIT = 7.0
SWIGLU_ALPHA = 1.702
LN_EPS = 1e-5
LANES = 128
VMEM_LIMIT = 56 * 1024 * 1024
DMA_THREADS = 2

F32 = jnp.float32
BF16 = jnp.bfloat16


def _params(*sem):
    return pltpu.CompilerParams(dimension_semantics=sem, vmem_limit_bytes=VMEM_LIMIT)


def _gelu(x):
    return 0.5 * x * (1.0 + lax.erf(x * (2.0 ** -0.5)))


def _layer_norm(z, g, b):
    mu = jnp.mean(z, axis=-1, keepdims=True)
    zc = z - mu
    var = jnp.mean(zc * zc, axis=-1, keepdims=True)
    return zc * lax.rsqrt(var + LN_EPS) * g + b


def _to_slab(ref, y):
    rows, d = y.shape
    nch = d // LANES
    for c in range(nch):
        ref[pl.ds(c, rows, stride=nch), :] = y[:, c * LANES:(c + 1) * LANES]


def _from_slab(ref, rows):
    nch = ref.shape[0] // rows
    return jnp.concatenate([ref[pl.ds(c, rows, stride=nch), :] for c in range(nch)], axis=1)


def _mm_kernel(x_ref, w_ref, b_ref, o_ref, wbf_ref, *, act):
    @pl.when(pl.program_id(1) == 0)
    def _():
        wbf_ref[...] = w_ref[...].astype(BF16)

    acc = jnp.dot(x_ref[...], wbf_ref[...], preferred_element_type=F32)
    acc = acc + b_ref[...]
    if act == "gelu":
        acc = _gelu(acc)
    o_ref[...] = acc.astype(o_ref.dtype)


def _matmul(x, w_all, layer, col0, n, b, *, act=None, out_dtype=BF16, tm=1024, tn=1024):
    m, k = x.shape
    tm, tn = min(tm, m), min(tn, n)
    assert m % tm == 0 and n % tn == 0 and col0 % tn == 0
    c0 = col0 // tn
    return pl.pallas_call(
        functools.partial(_mm_kernel, act=act),
        grid=(n // tn, m // tm),
        in_specs=[
            pl.BlockSpec((tm, k), lambda j, i: (i, 0)),
            pl.BlockSpec((None, k, tn), lambda j, i: (layer, 0, c0 + j)),
            pl.BlockSpec((1, tn), lambda j, i: (0, j)),
        ],
        out_specs=pl.BlockSpec((tm, tn), lambda j, i: (i, j)),
        out_shape=jax.ShapeDtypeStruct((m, n), out_dtype),
        scratch_shapes=[pltpu.VMEM((k, tn), BF16)],
        compiler_params=_params("parallel", "arbitrary"),
        name="matmul_bias_act",
    )(x, w_all, b.reshape(1, n).astype(F32))


def _mm_ln_kernel(*refs, alpha, residual, gelu, two_out):
    wbf_ref = refs[-1]
    refs = refs[:-1]
    if residual:
        h_ref, w_ref, res_ref, g_ref, b_ref = refs[:5]
        outs = refs[5:]
    else:
        h_ref, w_ref, bias_ref, g_ref, b_ref = refs[:5]
        outs = refs[5:]

    @pl.when(pl.program_id(0) == 0)
    def _():
        wbf_ref[...] = w_ref[...].astype(BF16)

    acc = jnp.dot(h_ref[...], wbf_ref[...], preferred_element_type=F32)
    if residual:
        z = alpha * res_ref[...] + acc
    else:
        z = acc + bias_ref[...]
    if gelu:
        z = _gelu(z)
    y = _layer_norm(z, g_ref[...], b_ref[...])
    if two_out:
        outs[0][...] = y
        outs[1][...] = y.astype(BF16)
    else:
        outs[0][...] = y.astype(outs[0].dtype)


def _weight_spec(k, n, layer, colblk):
    return pl.BlockSpec((None, k, n), lambda i: (layer, 0, colblk), pipeline_mode=pl.Buffered(1))


def _matmul_res_ln(h, w_all, layer, res, g, b, alpha, *, tm=256):
    m, k = h.shape
    n = w_all.shape[2]
    tm = min(tm, m)
    assert m % tm == 0
    row = lambda i: (i, 0)
    fix = lambda i: (0, 0)
    return pl.pallas_call(
        functools.partial(_mm_ln_kernel, alpha=alpha, residual=True, gelu=False, two_out=True),
        grid=(m // tm,),
        in_specs=[
            pl.BlockSpec((tm, k), row),
            _weight_spec(k, n, layer, 0),
            pl.BlockSpec((tm, n), row),
            pl.BlockSpec((1, n), fix),
            pl.BlockSpec((1, n), fix),
        ],
        out_specs=[pl.BlockSpec((tm, n), row), pl.BlockSpec((tm, n), row)],
        out_shape=[jax.ShapeDtypeStruct((m, n), F32), jax.ShapeDtypeStruct((m, n), BF16)],
        scratch_shapes=[pltpu.VMEM((k, n), BF16)],
        compiler_params=_params("arbitrary"),
        name="matmul_residual_ln",
    )(h, w_all, res, g.reshape(1, n), b.reshape(1, n))


def _matmul_gelu_ln(h, w_all, layer, colblk, n, bias, g, b, *, tm=256):
    m, k = h.shape
    tm = min(tm, m)
    assert m % tm == 0
    row = lambda i: (i, 0)
    fix = lambda i: (0, 0)
    return pl.pallas_call(
        functools.partial(_mm_ln_kernel, alpha=None, residual=False, gelu=True, two_out=False),
        grid=(m // tm,),
        in_specs=[
            pl.BlockSpec((tm, k), row),
            _weight_spec(k, n, layer, colblk),
            pl.BlockSpec((1, n), fix),
            pl.BlockSpec((1, n), fix),
            pl.BlockSpec((1, n), fix),
        ],
        out_specs=pl.BlockSpec((tm, n), row),
        out_shape=jax.ShapeDtypeStruct((m, n), BF16),
        scratch_shapes=[pltpu.VMEM((k, n), BF16)],
        compiler_params=_params("arbitrary"),
        name="matmul_gelu_ln",
    )(h, w_all, bias.reshape(1, n), g.reshape(1, n), b.reshape(1, n))


def _gates_kernel(w_ref, x_ref, b_ref, o_ref, *, heads):
    g = lax.dot_general(w_ref[...].astype(BF16), x_ref[...], (((0,), (1,)), ((), ())),
                        preferred_element_type=F32)
    g = g + b_ref[...]
    r = lax.broadcasted_iota(jnp.int32, g.shape, 0)
    is_forget = ((r >= heads) & (r < 2 * heads)) | (r >= 3 * heads)
    log_sig = jnp.minimum(g, 0.0) - jnp.log1p(jnp.exp(-jnp.abs(g)))
    o_ref[...] = jnp.where(is_forget, log_sig, g)


def _mlstm_gates(x_bf, w_gates, b_gates, *, tm=1024):
    t, k = x_bf.shape
    ng = w_gates.shape[1]
    tm = min(tm, t)
    assert t % tm == 0
    return pl.pallas_call(
        functools.partial(_gates_kernel, heads=ng // 4),
        grid=(t // tm,),
        in_specs=[
            pl.BlockSpec((k, ng), lambda i: (0, 0)),
            pl.BlockSpec((tm, k), lambda i: (i, 0)),
            pl.BlockSpec((ng, 1), lambda i: (0, 0)),
        ],
        out_specs=pl.BlockSpec((ng, tm), lambda i: (0, i)),
        out_shape=jax.ShapeDtypeStruct((ng, t), F32),
        compiler_params=_params("parallel"),
        name="mlstm_gates",
    )(w_gates, x_bf, b_gates.reshape(ng, 1).astype(F32))


def _mlstm_kernel(q_ref, k_ref, v_ref, o_ref, gt_ref, ng_ref, out_ref, hfw_ref, hbw_ref, cfw_ref, cbw_ref,
                  *, nc, scale):
    L = CHUNK
    dk = q_ref.shape[1]
    row_i = lax.broadcasted_iota(jnp.int32, (L, L), 0)
    col_i = lax.broadcasted_iota(jnp.int32, (L, L), 1)
    eye = row_i == col_i

    def chunk_step(c, carry, c_ref, reverse):
        n, m = carry
        off = pl.multiple_of(c * L, L)
        qc = q_ref[pl.ds(off, L), :]
        kc = k_ref[pl.ds(off, L), :]
        vc = v_ref[pl.ds(off, L), :]
        rows = gt_ref[0, 0, c]
        r0 = 2 if reverse else 0
        i_row = rows[r0:r0 + 1, :]
        f_row = rows[r0 + 1:r0 + 2, :]
        mask = (col_i >= row_i) if reverse else (col_i <= row_i)
        b_col = jnp.sum(jnp.where(mask, f_row, 0.0), axis=1, keepdims=True)
        b_row = jnp.sum(jnp.where(eye, b_col, 0.0), axis=0, keepdims=True)
        i_col = jnp.sum(jnp.where(eye, i_row, 0.0), axis=1, keepdims=True)
        dmat = jnp.where(mask, b_col - b_row + i_row, -jnp.inf)
        inter = b_col + m
        m_t = jnp.maximum(inter, jnp.max(dmat, axis=1, keepdims=True))
        p = jnp.exp(dmat - m_t)
        sqk = lax.dot_general(qc, kc, (((1,), (1,)), ((), ())), preferred_element_type=F32)
        s = sqk * (scale * p)
        isc = jnp.exp(inter - m_t)
        qn = jnp.sum(qc.astype(F32) * n, axis=1, keepdims=True)
        den = jnp.sum(s, axis=1, keepdims=True) + isc * (scale * qn)
        c_old = c_ref[...]
        num = jnp.dot(s.astype(BF16), vc, preferred_element_type=F32)
        num = num + (isc * scale) * jnp.dot(qc, c_old.astype(BF16), preferred_element_type=F32)
        h = num / jnp.maximum(jnp.abs(den), jnp.exp(-m_t))
        b_last = jnp.sum(f_row, axis=1, keepdims=True)
        g_col = b_last - b_col + i_col
        m_new = jnp.maximum(b_last + m, jnp.max(g_col, axis=0, keepdims=True))
        decay = jnp.exp(b_last + m - m_new)
        wk = jnp.exp(g_col - m_new)
        kw = kc.astype(F32) * wk
        c_ref[...] = decay * c_old + lax.dot_general(
            kw.astype(BF16), vc, (((0,), (0,)), ((), ())), preferred_element_type=F32)
        n_new = decay * n + jnp.sum(kw, axis=0, keepdims=True)
        return h, off, (n_new, m_new)

    init = (jnp.zeros((1, dk), F32), jnp.zeros((1, 1), F32))

    def scan_body(i, carry):
        fw, bw = carry
        h, off, fw = chunk_step(i, fw, cfw_ref, False)
        hfw_ref[pl.ds(off, L), :] = h
        h, off, bw = chunk_step(nc - 1 - i, bw, cbw_ref, True)
        hbw_ref[pl.ds(off, L), :] = h
        return fw, bw

    cfw_ref[...] = jnp.zeros_like(cfw_ref)
    cbw_ref[...] = jnp.zeros_like(cbw_ref)
    lax.fori_loop(0, nc, scan_body, (init, init), unroll=2)

    def finish_body(c, carry):
        off = pl.multiple_of(c * L, L)
        tot = hfw_ref[pl.ds(off, L), :] + hbw_ref[pl.ds(off, L), :]
        mu = jnp.mean(tot, axis=1, keepdims=True)
        hc = tot - mu
        var = jnp.mean(hc * hc, axis=1, keepdims=True)
        hn = hc * lax.rsqrt(var + LN_EPS) * ng_ref[...]
        og = jax.nn.sigmoid(o_ref[pl.ds(off, L), :].astype(F32))
        out_ref[pl.ds(off, L), :] = (og * hn).astype(out_ref.dtype)
        return carry

    lax.fori_loop(0, nc, finish_body, 0, unroll=4)


def _mlstm_scan(proj, gates_t, norm_g, *, batch, seq):
    H = MLSTM_HEADS
    L = CHUNK
    v_w = norm_g.shape[0]
    dv = v_w // H
    dk = (proj.shape[1] - 2 * v_w) // (2 * H)
    nc = seq // L
    assert seq % L == 0 and dv % LANES == 0 and dk % LANES == 0
    gt = gates_t.reshape(4, H, batch, nc, L).transpose(2, 1, 3, 0, 4)
    gt = jnp.pad(gt, ((0, 0), (0, 0), (0, 0), (0, 4), (0, 0)))
    k0 = H
    v0 = 2 * H * dk // dv
    o0 = v0 + H
    return pl.pallas_call(
        functools.partial(_mlstm_kernel, nc=nc, scale=float(dk) ** -0.5),
        grid=(batch, H),
        in_specs=[
            pl.BlockSpec((seq, dk), lambda b, h: (b, h)),
            pl.BlockSpec((seq, dk), lambda b, h: (b, k0 + h)),
            pl.BlockSpec((seq, dv), lambda b, h: (b, v0 + h)),
            pl.BlockSpec((seq, dv), lambda b, h: (b, o0 + h)),
            pl.BlockSpec((1, 1, nc, 8, L), lambda b, h: (b, h, 0, 0, 0)),
            pl.BlockSpec((1, dv), lambda b, h: (0, h)),
        ],
        out_specs=pl.BlockSpec((seq, dv), lambda b, h: (b, h)),
        out_shape=jax.ShapeDtypeStruct((batch * seq, v_w), BF16),
        scratch_shapes=[pltpu.VMEM((seq, dv), F32), pltpu.VMEM((seq, dv), F32),
                        pltpu.VMEM((dk, dv), F32), pltpu.VMEM((dk, dv), F32)],
        compiler_params=_params("parallel", "parallel"),
        name="mlstm_scan",
    )(proj, proj, proj, proj, gt, norm_g.reshape(1, v_w).astype(F32))


def _spatial_kernel(ws_ref, bs_ref, v_ref, u_ref, y_ref, *, groups):
    gd = v_ref.shape[1] // groups
    for g in range(groups):
        cols = slice(g * gd, (g + 1) * gd)
        s = jnp.dot(ws_ref[g], v_ref[:, cols], preferred_element_type=F32) + bs_ref[g]
        y_ref[:, cols] = (u_ref[:, cols].astype(F32) * s).astype(y_ref.dtype)


def _gmlp_spatial(u, vn, w_s, b_s):
    t, width = u.shape
    groups, L, _ = w_s.shape
    blk = pl.BlockSpec((L, width), lambda c: (c, 0))
    return pl.pallas_call(
        functools.partial(_spatial_kernel, groups=groups),
        grid=(t // L,),
        in_specs=[
            pl.BlockSpec((groups, L, L), lambda c: (0, 0, 0)),
            pl.BlockSpec((groups, L, 1), lambda c: (0, 0, 0)),
            blk, blk,
        ],
        out_specs=blk,
        out_shape=jax.ShapeDtypeStruct((t, width), BF16),
        compiler_params=_params("parallel"),
        name="gmlp_spatial",
    )(w_s.astype(BF16), b_s.reshape(groups, L, 1).astype(F32), vn, u)


def _router_kernel(x_ref, w_ref, b_ref, gate_ref, idx_ref, *, n_experts):
    x = x_ref[...]
    w = w_ref[...]
    xh = x.astype(BF16)
    xl = (x - xh.astype(F32)).astype(BF16)
    wh = w.astype(BF16)
    wl = (w - wh.astype(F32)).astype(BF16)
    logits = (jnp.dot(xh, wh, preferred_element_type=F32) + jnp.dot(xh, wl, preferred_element_type=F32)
              + jnp.dot(xl, wh, preferred_element_type=F32)) + b_ref[...]
    lane = lax.broadcasted_iota(jnp.int32, logits.shape, 1)
    cur = jnp.where(lane < n_experts, logits, -jnp.inf)
    vals, idxs = [], []
    for _ in range(TOP_K):
        mx = jnp.max(cur, axis=1, keepdims=True)
        ix = jnp.min(jnp.where(cur == mx, lane, LANES), axis=1, keepdims=True)
        vals.append(mx)
        idxs.append(ix)
        cur = jnp.where(lane == ix, -jnp.inf, cur)
    exps = [jnp.exp(v - vals[0]) for v in vals]
    tot = exps[0]
    for e in exps[1:]:
        tot = tot + e
    gate = jnp.zeros(logits.shape, F32)
    idx = jnp.zeros(logits.shape, jnp.int32)
    for k in range(TOP_K):
        gate = jnp.where(lane == k, exps[k] / tot, gate)
        idx = jnp.where(lane == k, idxs[k], idx)
    gate_ref[...] = gate
    idx_ref[...] = idx


def _router(x, rw, rb, *, tm=512):
    t, d = x.shape
    e = rw.shape[1]
    assert e <= LANES
    tm = min(tm, t)
    w_pad = jnp.pad(rw.astype(F32), ((0, 0), (0, LANES - e)))
    b_pad = jnp.pad(rb.astype(F32), (0, LANES - e)).reshape(1, LANES)
    return pl.pallas_call(
        functools.partial(_router_kernel, n_experts=e),
        grid=(t // tm,),
        in_specs=[
            pl.BlockSpec((tm, d), lambda i: (i, 0)),
            pl.BlockSpec((d, LANES), lambda i: (0, 0)),
            pl.BlockSpec((1, LANES), lambda i: (0, 0)),
        ],
        out_specs=[pl.BlockSpec((tm, LANES), lambda i: (i, 0))] * 2,
        out_shape=[jax.ShapeDtypeStruct((t, LANES), F32), jax.ShapeDtypeStruct((t, LANES), jnp.int32)],
        compiler_params=_params("parallel"),
        name="moe_router",
    )(x, w_pad, b_pad)


RANK_BLOCK = 512


def _rank_kernel(e_ref, rank_ref, cnt_ref, carry_ref):
    @pl.when(pl.program_id(0) == 0)
    def _():
        carry_ref[...] = jnp.zeros_like(carry_ref)

    e_row = e_ref[0]
    ex = lax.broadcasted_iota(jnp.int32, (LANES, RANK_BLOCK), 0)
    onehot = e_row == ex
    a_i = lax.broadcasted_iota(jnp.int32, (RANK_BLOCK, RANK_BLOCK), 0)
    b_i = lax.broadcasted_iota(jnp.int32, (RANK_BLOCK, RANK_BLOCK), 1)
    before = (a_i < b_i).astype(BF16)
    within = jnp.dot(onehot.astype(BF16), before, preferred_element_type=F32)
    carry = carry_ref[...][:, :1]
    rank = jnp.sum(jnp.where(onehot, within + carry, 0.0), axis=0, keepdims=True)
    rank_ref[0] = rank.astype(jnp.int32)
    new = carry + jnp.sum(onehot.astype(F32), axis=1, keepdims=True)
    carry_ref[...] = jnp.broadcast_to(new, carry_ref.shape)
    cnt_ref[...] = jnp.broadcast_to(new, cnt_ref.shape).astype(jnp.int32)


def _expert_ranks(e_flat, n_experts):
    a = e_flat.shape[0]
    assert a % RANK_BLOCK == 0 and n_experts <= LANES
    nb = a // RANK_BLOCK
    rank, cnt = pl.pallas_call(
        _rank_kernel,
        grid=(nb,),
        in_specs=[pl.BlockSpec((1, 1, RANK_BLOCK), lambda i: (i, 0, 0))],
        out_specs=[pl.BlockSpec((1, 1, RANK_BLOCK), lambda i: (i, 0, 0)),
                   pl.BlockSpec((LANES, LANES), lambda i: (0, 0))],
        out_shape=[jax.ShapeDtypeStruct((nb, 1, RANK_BLOCK), jnp.int32),
                   jax.ShapeDtypeStruct((LANES, LANES), jnp.int32)],
        scratch_shapes=[pltpu.VMEM((LANES, LANES), F32)],
        compiler_params=_params("arbitrary"),
        name="moe_rank",
    )(e_flat.reshape(nb, 1, RANK_BLOCK))
    return rank.reshape(a), cnt[:n_experts, 0]


INVERT_CHUNK = 4096


def _invert_kernel(dest_ref, lo_ref, hi_ref, out_ref, *, chunk, n_exp, n_slots):
    step = pl.program_id(0)

    def fill_range(lo, hi):
        def fill(s, c):
            out_ref[s] = -1
            return c
        lax.fori_loop(lo, hi, fill, 0)

    @pl.when(step == 0)
    def _():
        for e in range(n_exp):
            fill_range(lo_ref[e], hi_ref[e])
        fill_range(hi_ref[n_exp - 1], n_slots)

    @pl.when(step > 0)
    def _():
        base = (step - 1) * chunk

        def put(i, c):
            a = base + i
            out_ref[dest_ref[a]] = a
            return c
        lax.fori_loop(0, chunk, put, 0, unroll=8)


def _invert_dest(dest, pad_lo, pad_hi, n_slots):
    a = dest.shape[0]
    chunk = min(INVERT_CHUNK, a)
    assert a % chunk == 0
    smem = pl.BlockSpec(memory_space=pltpu.SMEM)
    return pl.pallas_call(
        functools.partial(_invert_kernel, chunk=chunk, n_exp=pad_lo.shape[0], n_slots=n_slots),
        grid=(a // chunk + 1,),
        in_specs=[smem, smem, smem],
        out_specs=smem,
        out_shape=jax.ShapeDtypeStruct((n_slots,), jnp.int32),
        compiler_params=_params("arbitrary"),
        name="moe_invert",
    )(dest, pad_lo, pad_hi)


def _expert_kernel(be_ref, dst_ref, x_hbm, wgu_ref, bgu_ref, wdn_ref, bdn_ref, out_hbm,
                   xbuf0, xbuf1, ybuf0, ybuf1, hbuf, wgu_bf, wdn_bf, gsem, ssem, *, n_tokens, n_blocks, ff):
    j = pl.program_id(0)
    R = MOE_BLOCK
    xbufs = (xbuf0, xbuf1)
    ybufs = (ybuf0, ybuf1)

    @pl.when((j == 0) | (be_ref[j] != be_ref[jnp.maximum(j - 1, 0)]))
    def _():
        wgu_bf[...] = wgu_ref[...].astype(BF16)
        wdn_bf[...] = wdn_ref[...].astype(BF16)

    def start_gather(blk, s):
        for r in range(R):
            tok = dst_ref[blk, r] & (n_tokens - 1)
            pltpu.make_async_copy(x_hbm.at[pl.ds(tok, 1)], xbufs[s].at[pl.ds(r, 1)],
                                  gsem.at[s]).start(priority=r % DMA_THREADS)

    def wait_gather(s):
        pltpu.make_async_copy(x_hbm.at[pl.ds(0, R)], xbufs[s], gsem.at[s]).wait()

    nch = ybuf0.shape[0] // R

    def start_scatter(blk, s):
        for r in range(R):
            dst = pl.multiple_of(dst_ref[blk, r] * nch, nch)
            pltpu.make_async_copy(ybufs[s].at[pl.ds(r * nch, nch)], out_hbm.at[pl.ds(dst, nch)],
                                  ssem.at[s]).start(priority=r % DMA_THREADS)

    def wait_scatter(s):
        pltpu.make_async_copy(ybufs[s], out_hbm.at[pl.ds(0, R * nch)], ssem.at[s]).wait()

    def up_proj(s):
        xb = xbufs[s][...].astype(BF16)
        hgu = jnp.dot(xb, wgu_bf[...], preferred_element_type=F32) + bgu_ref[0]
        g = jnp.minimum(hgu[:, :ff], SWIGLU_LIMIT)
        u = jnp.clip(hgu[:, ff:], -SWIGLU_LIMIT, SWIGLU_LIMIT)
        hbuf[...] = ((u + 1.0) * (g * jax.nn.sigmoid(g * SWIGLU_ALPHA))).astype(BF16)

    def down_proj(s):
        _to_slab(ybufs[s], jnp.dot(hbuf[...], wdn_bf[...], preferred_element_type=F32) + bdn_ref[0])

    @pl.when(j == 0)
    def _():
        start_gather(0, 0)
        wait_gather(0)
        start_gather(1, 1)
        up_proj(0)
        down_proj(0)

    for s in (0, 1):
        @pl.when((j > 0) & (lax.rem(j, 2) == s))
        def _(s=s):
            wait_gather(s)
            start_gather(jnp.minimum(j + 1, n_blocks - 1), 1 - s)
            up_proj(s)

            @pl.when(j >= 2)
            def _():
                wait_scatter(s)

            start_scatter(j - 1, 1 - s)
            down_proj(s)

    @pl.when(j == n_blocks - 1)
    def _():
        s_last = (n_blocks - 1) % 2
        start_scatter(n_blocks - 1, s_last)
        wait_gather(1 - s_last)
        wait_scatter(1 - s_last)
        wait_scatter(s_last)


def _expert_blocks(x, block_e, dst_rows, w_gu_all, w_dn_all, layer, b_gu, b_dn):
    t, d = x.shape
    n_blocks = block_e.shape[0]
    _, n_exp, _, ff2 = w_gu_all.shape
    ff = ff2 // 2
    assert t & (t - 1) == 0, "token count must be a power of two"
    assert n_blocks >= 2
    wmap = lambda j, be, ds: (layer, be[j], 0, 0)
    bmap = lambda j, be, ds: (be[j], 0, 0)
    return pl.pallas_call(
        functools.partial(_expert_kernel, n_tokens=t, n_blocks=n_blocks, ff=ff),
        grid_spec=pltpu.PrefetchScalarGridSpec(
            num_scalar_prefetch=2,
            grid=(n_blocks,),
            in_specs=[
                pl.BlockSpec(memory_space=pl.ANY),
                pl.BlockSpec((None, None, d, ff2), wmap),
                pl.BlockSpec((1, 1, ff2), bmap),
                pl.BlockSpec((None, None, ff, d), wmap),
                pl.BlockSpec((1, 1, d), bmap),
            ],
            out_specs=pl.BlockSpec(memory_space=pl.ANY),
            scratch_shapes=[
                pltpu.VMEM((MOE_BLOCK, d), F32),
                pltpu.VMEM((MOE_BLOCK, d), F32),
                pltpu.VMEM((MOE_BLOCK * (d // LANES), LANES), F32),
                pltpu.VMEM((MOE_BLOCK * (d // LANES), LANES), F32),
                pltpu.VMEM((MOE_BLOCK, ff), BF16),
                pltpu.VMEM((d, ff2), BF16),
                pltpu.VMEM((ff, d), BF16),
                pltpu.SemaphoreType.DMA((2,)),
                pltpu.SemaphoreType.DMA((2,)),
            ],
        ),
        out_shape=jax.ShapeDtypeStruct((n_blocks * MOE_BLOCK * (d // LANES), LANES), F32),
        compiler_params=_params("arbitrary"),
        name="moe_experts",
    )(block_e, dst_rows, x, w_gu_all, b_gu.reshape(n_exp, 1, ff2), w_dn_all, b_dn.reshape(n_exp, 1, d))


def _combine_kernel(x_ref, y0_ref, y1_ref, y2_ref, y3_ref, gate_ref, g_ref, b_ref, o_ref, obf_ref, *, alpha):
    gate = gate_ref[...]
    rows = x_ref.shape[0]
    ffn = gate[:, 0:1] * _from_slab(y0_ref, rows)
    for k, y_ref in enumerate((y1_ref, y2_ref, y3_ref), start=1):
        ffn = ffn + gate[:, k:k + 1] * _from_slab(y_ref, rows)
    y = _layer_norm(alpha * x_ref[...] + ffn, g_ref[...], b_ref[...])
    o_ref[...] = y
    obf_ref[...] = y.astype(BF16)


def _combine_ln(x, ys, gate_pad, g, b, alpha, *, tm=256):
    t, d = x.shape
    tm = min(tm, t)
    nt = t // tm
    row = lambda i: (i, 0)
    fix = lambda i: (0, 0)
    nch = d // LANES
    y_specs = [pl.BlockSpec((tm * nch, LANES), functools.partial(lambda i, k: (i + k * nt, 0), k=k))
               for k in range(TOP_K)]
    return pl.pallas_call(
        functools.partial(_combine_kernel, alpha=alpha),
        grid=(nt,),
        in_specs=[pl.BlockSpec((tm, d), row)] + y_specs + [
            pl.BlockSpec((tm, LANES), row),
            pl.BlockSpec((1, d), fix),
            pl.BlockSpec((1, d), fix),
        ],
        out_specs=[pl.BlockSpec((tm, d), row), pl.BlockSpec((tm, d), row)],
        out_shape=[jax.ShapeDtypeStruct((t, d), F32), jax.ShapeDtypeStruct((t, d), BF16)],
        compiler_params=_params("parallel"),
        name="moe_combine_ln",
    )(x, ys, ys, ys, ys, gate_pad, g.reshape(1, d), b.reshape(1, d))


def _moe_layer(x, layer, rw, rb, w_gu_all, b_gu, w_dn_all, b_dn, g, b, alpha):
    t, d = x.shape
    n_exp = rw.shape[1]
    a = t * TOP_K
    gate_pad, idx_pad = _router(x, rw, rb)
    e_flat = idx_pad[:, :TOP_K].reshape(a)
    rank, counts = _expert_ranks(e_flat, n_exp)
    padded = (counts + MOE_BLOCK - 1) // MOE_BLOCK * MOE_BLOCK
    pend = jnp.cumsum(padded)
    pstart = pend - padded
    dest = pstart[e_flat] + rank
    n_slots = a + n_exp * MOE_BLOCK
    n_blocks = n_slots // MOE_BLOCK
    block_start = jnp.arange(n_blocks, dtype=jnp.int32) * MOE_BLOCK
    block_e = jnp.minimum(jnp.sum((block_start[:, None] >= pend[None, :]).astype(jnp.int32), axis=1), n_exp - 1)
    slot_a = _invert_dest(dest, pstart + counts, pend, n_slots)
    valid = slot_a >= 0
    pad_idx = jnp.cumsum(jnp.logical_not(valid).astype(jnp.int32)) - 1
    dst_rows = jnp.where(valid, (slot_a % TOP_K) * t + slot_a // TOP_K, a + pad_idx).astype(jnp.int32)
    ys = _expert_blocks(x, block_e, dst_rows.reshape(n_blocks, MOE_BLOCK), w_gu_all, w_dn_all, layer, b_gu, b_dn)
    return _combine_ln(x, ys, gate_pad, g, b, alpha)


def _mlstm_layer(x, x_bf, w_in_all, j, b_gates, norm_g, w_out_all, g, b, alpha, *, batch, seq):
    H = MLSTM_HEADS
    n_main = w_in_all.shape[2] - 4 * H
    proj = _matmul(x_bf, w_in_all, j, 0, n_main, jnp.zeros((n_main,), F32))
    gates_t = _mlstm_gates(x_bf, w_in_all[j, :, n_main:], b_gates)
    h = _mlstm_scan(proj, gates_t, norm_g, batch=batch, seq=seq)
    return _matmul_res_ln(h, w_out_all, j, x, g, b, alpha)


def _gmlp_layer(x, x_bf, w_in_all, j, b_in, vn_g, vn_b, w_s, b_s, w_out_all, g, b, alpha):
    width = w_out_all.shape[1]
    u = _matmul(x_bf, w_in_all, j, 0, width, b_in[:width], act="gelu")
    vn = _matmul_gelu_ln(x_bf, w_in_all, j, 1, width, b_in[width:], vn_g, vn_b)
    y = _gmlp_spatial(u, vn, w_s, b_s)
    return _matmul_res_ln(y, w_out_all, j, x, g, b, alpha)


def kernel(x, mlstm_w_in, mlstm_b_gates, mlstm_norm_g, mlstm_w_out, gmlp_w_in, gmlp_b_in, gmlp_vnorm_g,
           gmlp_vnorm_b, gmlp_w_s, gmlp_b_s, gmlp_w_out, router_w, router_b, expert_w_gu, expert_b_gu,
           expert_w_down, expert_b_down, ln_g, ln_b):
    batch, seq, d = x.shape
    depth = ln_g.shape[0]
    alpha = float((2 * depth) ** 0.25)
    xf = x.reshape(batch * seq, d)
    x_bf = xf.astype(BF16)
    for layer in range(depth):
        j = layer // 2
        if layer % 2 == 0:
            xf, x_bf = _mlstm_layer(xf, x_bf, mlstm_w_in, j, mlstm_b_gates[j], mlstm_norm_g[j], mlstm_w_out,
                                    ln_g[layer, 0], ln_b[layer, 0], alpha, batch=batch, seq=seq)
        else:
            xf, x_bf = _gmlp_layer(xf, x_bf, gmlp_w_in, j, gmlp_b_in[j], gmlp_vnorm_g[j], gmlp_vnorm_b[j],
                                   gmlp_w_s[j], gmlp_b_s[j], gmlp_w_out,
                                   ln_g[layer, 0], ln_b[layer, 0], alpha)
        xf, x_bf = _moe_layer(xf, layer, router_w[layer], router_b[layer],
                              expert_w_gu, expert_b_gu[layer], expert_w_down, expert_b_down[layer],
                              ln_g[layer, 1], ln_b[layer, 1], alpha)
    return xf.reshape(batch, seq, d)
```

```python
import functools

import jax
import jax.numpy as jnp
from jax import lax
from jax.experimental import pallas as pl
from jax.experimental.pallas import tpu as pltpu

MLSTM_HEADS = 8
CHUNK = 128
GMLP_GROUPS = 8
TOP_K = 4
MOE_BLOCK = 128
SWIGLU_LIMIT = 7.0
SWIGLU_ALPHA = 1.702
LN_EPS = 1e-5
LANES = 128
VMEM_LIMIT = 56 * 1024 * 1024

F32 = jnp.float32
BF16 = jnp.bfloat16


def _params(*sem):
    return pltpu.CompilerParams(dimension_semantics=sem, vmem_limit_bytes=VMEM_LIMIT)


def _gelu(x):
    return 0.5 * x * (1.0 + lax.erf(x * (2.0 ** -0.5)))


def _layer_norm(z, g, b):
    mu = jnp.mean(z, axis=-1, keepdims=True)
    zc = z - mu
    var = jnp.mean(zc * zc, axis=-1, keepdims=True)
    return zc * lax.rsqrt(var + LN_EPS) * g + b


def _mm_kernel(x_ref, w_ref, b_ref, o_ref, wbf_ref, *, act):
    @pl.when(pl.program_id(1) == 0)
    def _():
        wbf_ref[...] = w_ref[...].astype(BF16)

    acc = jnp.dot(x_ref[...], wbf_ref[...], preferred_element_type=F32)
    acc = acc + b_ref[...]
    if act == "gelu":
        acc = _gelu(acc)
    o_ref[...] = acc.astype(o_ref.dtype)


def _matmul(x, w_all, layer, col0, n, b, *, act=None, out_dtype=BF16, tm=1024, tn=1024):
    m, k = x.shape
    tm, tn = min(tm, m), min(tn, n)
    assert m % tm == 0 and n % tn == 0 and col0 % tn == 0
    c0 = col0 // tn
    return pl.pallas_call(
        functools.partial(_mm_kernel, act=act),
        grid=(n // tn, m // tm),
        in_specs=[
            pl.BlockSpec((tm, k), lambda j, i: (i, 0)),
            pl.BlockSpec((None, k, tn), lambda j, i: (layer, 0, c0 + j)),
            pl.BlockSpec((1, tn), lambda j, i: (0, j)),
        ],
        out_specs=pl.BlockSpec((tm, tn), lambda j, i: (i, j)),
        out_shape=jax.ShapeDtypeStruct((m, n), out_dtype),
        scratch_shapes=[pltpu.VMEM((k, tn), BF16)],
        compiler_params=_params("parallel", "arbitrary"),
        name="matmul_bias_act",
    )(x, w_all, b.reshape(1, n).astype(F32))


def _mm_ln_kernel(*refs, alpha, residual, gelu, two_out):
    wbf_ref = refs[-1]
    refs = refs[:-1]
    if residual:
        h_ref, w_ref, res_ref, g_ref, b_ref = refs[:5]
        outs = refs[5:]
    else:
        h_ref, w_ref, bias_ref, g_ref, b_ref = refs[:5]
        outs = refs[5:]

    @pl.when(pl.program_id(0) == 0)
    def _():
        wbf_ref[...] = w_ref[...].astype(BF16)

    acc = jnp.dot(h_ref[...], wbf_ref[...], preferred_element_type=F32)
    if residual:
        z = alpha * res_ref[...] + acc
    else:
        z = acc + bias_ref[...]
    if gelu:
        z = _gelu(z)
    y = _layer_norm(z, g_ref[...], b_ref[...])
    if two_out:
        outs[0][...] = y
        outs[1][...] = y.astype(BF16)
    else:
        outs[0][...] = y.astype(outs[0].dtype)


def _weight_spec(k, n, layer, colblk):
    return pl.BlockSpec((None, k, n), lambda i: (layer, 0, colblk), pipeline_mode=pl.Buffered(1))


def _matmul_res_ln(h, w_all, layer, res, g, b, alpha, *, tm=256):
    m, k = h.shape
    n = w_all.shape[2]
    tm = min(tm, m)
    assert m % tm == 0
    row = lambda i: (i, 0)
    fix = lambda i: (0, 0)
    return pl.pallas_call(
        functools.partial(_mm_ln_kernel, alpha=alpha, residual=True, gelu=False, two_out=True),
        grid=(m // tm,),
        in_specs=[
            pl.BlockSpec((tm, k), row),
            _weight_spec(k, n, layer, 0),
            pl.BlockSpec((tm, n), row),
            pl.BlockSpec((1, n), fix),
            pl.BlockSpec((1, n), fix),
        ],
        out_specs=[pl.BlockSpec((tm, n), row), pl.BlockSpec((tm, n), row)],
        out_shape=[jax.ShapeDtypeStruct((m, n), F32), jax.ShapeDtypeStruct((m, n), BF16)],
        scratch_shapes=[pltpu.VMEM((k, n), BF16)],
        compiler_params=_params("arbitrary"),
        name="matmul_residual_ln",
    )(h, w_all, res, g.reshape(1, n), b.reshape(1, n))


def _matmul_gelu_ln(h, w_all, layer, colblk, n, bias, g, b, *, tm=256):
    m, k = h.shape
    tm = min(tm, m)
    assert m % tm == 0
    row = lambda i: (i, 0)
    fix = lambda i: (0, 0)
    return pl.pallas_call(
        functools.partial(_mm_ln_kernel, alpha=None, residual=False, gelu=True, two_out=False),
        grid=(m // tm,),
        in_specs=[
            pl.BlockSpec((tm, k), row),
            _weight_spec(k, n, layer, colblk),
            pl.BlockSpec((1, n), fix),
            pl.BlockSpec((1, n), fix),
            pl.BlockSpec((1, n), fix),
        ],
        out_specs=pl.BlockSpec((tm, n), row),
        out_shape=jax.ShapeDtypeStruct((m, n), BF16),
        scratch_shapes=[pltpu.VMEM((k, n), BF16)],
        compiler_params=_params("arbitrary"),
        name="matmul_gelu_ln",
    )(h, w_all, bias.reshape(1, n), g.reshape(1, n), b.reshape(1, n))


def _gates_kernel(w_ref, x_ref, b_ref, o_ref, *, heads):
    g = lax.dot_general(w_ref[...].astype(BF16), x_ref[...], (((0,), (1,)), ((), ())),
                        preferred_element_type=F32)
    g = g + b_ref[...]
    r = lax.broadcasted_iota(jnp.int32, g.shape, 0)
    is_forget = ((r >= heads) & (r < 2 * heads)) | (r >= 3 * heads)
    log_sig = jnp.minimum(g, 0.0) - jnp.log1p(jnp.exp(-jnp.abs(g)))
    o_ref[...] = jnp.where(is_forget, log_sig, g)


def _mlstm_gates(x_bf, w_gates, b_gates, *, tm=1024):
    t, k = x_bf.shape
    ng = w_gates.shape[1]
    tm = min(tm, t)
    assert t % tm == 0
    return pl.pallas_call(
        functools.partial(_gates_kernel, heads=ng // 4),
        grid=(t // tm,),
        in_specs=[
            pl.BlockSpec((k, ng), lambda i: (0, 0)),
            pl.BlockSpec((tm, k), lambda i: (i, 0)),
            pl.BlockSpec((ng, 1), lambda i: (0, 0)),
        ],
        out_specs=pl.BlockSpec((ng, tm), lambda i: (0, i)),
        out_shape=jax.ShapeDtypeStruct((ng, t), F32),
        compiler_params=_params("parallel"),
        name="mlstm_gates",
    )(w_gates, x_bf, b_gates.reshape(ng, 1).astype(F32))


def _mlstm_kernel(q_ref, k_ref, v_ref, o_ref, gt_ref, ng_ref, out_ref, hfw_ref, hbw_ref, cfw_ref, cbw_ref,
                  *, nc, scale):
    L = CHUNK
    dk = q_ref.shape[1]
    row_i = lax.broadcasted_iota(jnp.int32, (L, L), 0)
    col_i = lax.broadcasted_iota(jnp.int32, (L, L), 1)
    eye = row_i == col_i

    def chunk_step(c, carry, c_ref, reverse):
        n, m = carry
        off = pl.multiple_of(c * L, L)
        qc = q_ref[pl.ds(off, L), :]
        kc = k_ref[pl.ds(off, L), :]
        vc = v_ref[pl.ds(off, L), :]
        rows = gt_ref[0, 0, c]
        r0 = 2 if reverse else 0
        i_row = rows[r0:r0 + 1, :]
        f_row = rows[r0 + 1:r0 + 2, :]
        mask = (col_i >= row_i) if reverse else (col_i <= row_i)
        b_col = jnp.sum(jnp.where(mask, f_row, 0.0), axis=1, keepdims=True)
        b_row = jnp.sum(jnp.where(eye, b_col, 0.0), axis=0, keepdims=True)
        i_col = jnp.sum(jnp.where(eye, i_row, 0.0), axis=1, keepdims=True)
        dmat = jnp.where(mask, b_col - b_row + i_row, -jnp.inf)
        inter = b_col + m
        m_t = jnp.maximum(inter, jnp.max(dmat, axis=1, keepdims=True))
        p = jnp.exp(dmat - m_t)
        sqk = lax.dot_general(qc, kc, (((1,), (1,)), ((), ())), preferred_element_type=F32)
        s = sqk * (scale * p)
        isc = jnp.exp(inter - m_t)
        qn = jnp.sum(qc.astype(F32) * n, axis=1, keepdims=True)
        den = jnp.sum(s, axis=1, keepdims=True) + isc * (scale * qn)
        c_old = c_ref[...]
        num = jnp.dot(s.astype(BF16), vc, preferred_element_type=F32)
        num = num + (isc * scale) * jnp.dot(qc, c_old.astype(BF16), preferred_element_type=F32)
        h = num / jnp.maximum(jnp.abs(den), jnp.exp(-m_t))
        b_last = jnp.sum(f_row, axis=1, keepdims=True)
        g_col = b_last - b_col + i_col
        m_new = jnp.maximum(b_last + m, jnp.max(g_col, axis=0, keepdims=True))
        decay = jnp.exp(b_last + m - m_new)
        wk = jnp.exp(g_col - m_new)
        kw = kc.astype(F32) * wk
        c_ref[...] = decay * c_old + lax.dot_general(
            kw.astype(BF16), vc, (((0,), (0,)), ((), ())), preferred_element_type=F32)
        n_new = decay * n + jnp.sum(kw, axis=0, keepdims=True)
        return h, off, (n_new, m_new)

    init = (jnp.zeros((1, dk), F32), jnp.zeros((1, 1), F32))

    def scan_body(i, carry):
        fw, bw = carry
        h, off, fw = chunk_step(i, fw, cfw_ref, False)
        hfw_ref[pl.ds(off, L), :] = h
        h, off, bw = chunk_step(nc - 1 - i, bw, cbw_ref, True)
        hbw_ref[pl.ds(off, L), :] = h
        return fw, bw

    cfw_ref[...] = jnp.zeros_like(cfw_ref)
    cbw_ref[...] = jnp.zeros_like(cbw_ref)
    lax.fori_loop(0, nc, scan_body, (init, init), unroll=2)

    def finish_body(c, carry):
        off = pl.multiple_of(c * L, L)
        tot = hfw_ref[pl.ds(off, L), :] + hbw_ref[pl.ds(off, L), :]
        mu = jnp.mean(tot, axis=1, keepdims=True)
        hc = tot - mu
        var = jnp.mean(hc * hc, axis=1, keepdims=True)
        hn = hc * lax.rsqrt(var + LN_EPS) * ng_ref[...]
        og = jax.nn.sigmoid(o_ref[pl.ds(off, L), :].astype(F32))
        out_ref[pl.ds(off, L), :] = (og * hn).astype(out_ref.dtype)
        return carry

    lax.fori_loop(0, nc, finish_body, 0, unroll=4)


def _mlstm_scan(proj, gates_t, norm_g, *, batch, seq):
    H = MLSTM_HEADS
    L = CHUNK
    v_w = norm_g.shape[0]
    dv = v_w // H
    dk = (proj.shape[1] - 2 * v_w) // (2 * H)
    nc = seq // L
    assert seq % L == 0 and dv % LANES == 0 and dk % LANES == 0
    gt = gates_t.reshape(4, H, batch, nc, L).transpose(2, 1, 3, 0, 4)
    gt = jnp.pad(gt, ((0, 0), (0, 0), (0, 0), (0, 4), (0, 0)))
    k0 = H
    v0 = 2 * H * dk // dv
    o0 = v0 + H
    return pl.pallas_call(
        functools.partial(_mlstm_kernel, nc=nc, scale=float(dk) ** -0.5),
        grid=(batch, H),
        in_specs=[
            pl.BlockSpec((seq, dk), lambda b, h: (b, h)),
            pl.BlockSpec((seq, dk), lambda b, h: (b, k0 + h)),
            pl.BlockSpec((seq, dv), lambda b, h: (b, v0 + h)),
            pl.BlockSpec((seq, dv), lambda b, h: (b, o0 + h)),
            pl.BlockSpec((1, 1, nc, 8, L), lambda b, h: (b, h, 0, 0, 0)),
            pl.BlockSpec((1, dv), lambda b, h: (0, h)),
        ],
        out_specs=pl.BlockSpec((seq, dv), lambda b, h: (b, h)),
        out_shape=jax.ShapeDtypeStruct((batch * seq, v_w), BF16),
        scratch_shapes=[pltpu.VMEM((seq, dv), F32), pltpu.VMEM((seq, dv), F32),
                        pltpu.VMEM((dk, dv), F32), pltpu.VMEM((dk, dv), F32)],
        compiler_params=_params("parallel", "parallel"),
        name="mlstm_scan",
    )(proj, proj, proj, proj, gt, norm_g.reshape(1, v_w).astype(F32))


def _spatial_kernel(ws_ref, bs_ref, v_ref, u_ref, y_ref, *, groups):
    gd = v_ref.shape[1] // groups
    for g in range(groups):
        cols = slice(g * gd, (g + 1) * gd)
        s = jnp.dot(ws_ref[g], v_ref[:, cols], preferred_element_type=F32) + bs_ref[g]
        y_ref[:, cols] = (u_ref[:, cols].astype(F32) * s).astype(y_ref.dtype)


def _gmlp_spatial(u, vn, w_s, b_s):
    t, width = u.shape
    groups, L, _ = w_s.shape
    blk = pl.BlockSpec((L, width), lambda c: (c, 0))
    return pl.pallas_call(
        functools.partial(_spatial_kernel, groups=groups),
        grid=(t // L,),
        in_specs=[
            pl.BlockSpec((groups, L, L), lambda c: (0, 0, 0)),
            pl.BlockSpec((groups, L, 1), lambda c: (0, 0, 0)),
            blk, blk,
        ],
        out_specs=blk,
        out_shape=jax.ShapeDtypeStruct((t, width), BF16),
        compiler_params=_params("parallel"),
        name="gmlp_spatial",
    )(w_s.astype(BF16), b_s.reshape(groups, L, 1).astype(F32), vn, u)


def _router_kernel(x_ref, w_ref, b_ref, gate_ref, idx_ref, *, n_experts):
    x = x_ref[...]
    w = w_ref[...]
    xh = x.astype(BF16)
    xl = (x - xh.astype(F32)).astype(BF16)
    wh = w.astype(BF16)
    wl = (w - wh.astype(F32)).astype(BF16)
    logits = (jnp.dot(xh, wh, preferred_element_type=F32) + jnp.dot(xh, wl, preferred_element_type=F32)
              + jnp.dot(xl, wh, preferred_element_type=F32)) + b_ref[...]
    lane = lax.broadcasted_iota(jnp.int32, logits.shape, 1)
    cur = jnp.where(lane < n_experts, logits, -jnp.inf)
    vals, idxs = [], []
    for _ in range(TOP_K):
        mx = jnp.max(cur, axis=1, keepdims=True)
        ix = jnp.min(jnp.where(cur == mx, lane, LANES), axis=1, keepdims=True)
        vals.append(mx)
        idxs.append(ix)
        cur = jnp.where(lane == ix, -jnp.inf, cur)
    exps = [jnp.exp(v - vals[0]) for v in vals]
    tot = exps[0]
    for e in exps[1:]:
        tot = tot + e
    gate = jnp.zeros(logits.shape, F32)
    idx = jnp.zeros(logits.shape, jnp.int32)
    for k in range(TOP_K):
        gate = jnp.where(lane == k, exps[k] / tot, gate)
        idx = jnp.where(lane == k, idxs[k], idx)
    gate_ref[...] = gate
    idx_ref[...] = idx


def _router(x, rw, rb, *, tm=512):
    t, d = x.shape
    e = rw.shape[1]
    assert e <= LANES
    tm = min(tm, t)
    w_pad = jnp.pad(rw.astype(F32), ((0, 0), (0, LANES - e)))
    b_pad = jnp.pad(rb.astype(F32), (0, LANES - e)).reshape(1, LANES)
    return pl.pallas_call(
        functools.partial(_router_kernel, n_experts=e),
        grid=(t // tm,),
        in_specs=[
            pl.BlockSpec((tm, d), lambda i: (i, 0)),
            pl.BlockSpec((d, LANES), lambda i: (0, 0)),
            pl.BlockSpec((1, LANES), lambda i: (0, 0)),
        ],
        out_specs=[pl.BlockSpec((tm, LANES), lambda i: (i, 0))] * 2,
        out_shape=[jax.ShapeDtypeStruct((t, LANES), F32), jax.ShapeDtypeStruct((t, LANES), jnp.int32)],
        compiler_params=_params("parallel"),
        name="moe_router",
    )(x, w_pad, b_pad)


RANK_BLOCK = 512


def _rank_kernel(e_ref, rank_ref, cnt_ref, carry_ref):
    @pl.when(pl.program_id(0) == 0)
    def _():
        carry_ref[...] = jnp.zeros_like(carry_ref)

    e_row = e_ref[0]
    ex = lax.broadcasted_iota(jnp.int32, (LANES, RANK_BLOCK), 0)
    onehot = e_row == ex
    a_i = lax.broadcasted_iota(jnp.int32, (RANK_BLOCK, RANK_BLOCK), 0)
    b_i = lax.broadcasted_iota(jnp.int32, (RANK_BLOCK, RANK_BLOCK), 1)
    before = (a_i < b_i).astype(BF16)
    within = jnp.dot(onehot.astype(BF16), before, preferred_element_type=F32)
    carry = carry_ref[...][:, :1]
    rank = jnp.sum(jnp.where(onehot, within + carry, 0.0), axis=0, keepdims=True)
    rank_ref[0] = rank.astype(jnp.int32)
    new = carry + jnp.sum(onehot.astype(F32), axis=1, keepdims=True)
    carry_ref[...] = jnp.broadcast_to(new, carry_ref.shape)
    cnt_ref[...] = jnp.broadcast_to(new, cnt_ref.shape).astype(jnp.int32)


def _expert_ranks(e_flat, n_experts):
    a = e_flat.shape[0]
    assert a % RANK_BLOCK == 0 and n_experts <= LANES
    nb = a // RANK_BLOCK
    rank, cnt = pl.pallas_call(
        _rank_kernel,
        grid=(nb,),
        in_specs=[pl.BlockSpec((1, 1, RANK_BLOCK), lambda i: (i, 0, 0))],
        out_specs=[pl.BlockSpec((1, 1, RANK_BLOCK), lambda i: (i, 0, 0)),
                   pl.BlockSpec((LANES, LANES), lambda i: (0, 0))],
        out_shape=[jax.ShapeDtypeStruct((nb, 1, RANK_BLOCK), jnp.int32),
                   jax.ShapeDtypeStruct((LANES, LANES), jnp.int32)],
        scratch_shapes=[pltpu.VMEM((LANES, LANES), F32)],
        compiler_params=_params("arbitrary"),
        name="moe_rank",
    )(e_flat.reshape(nb, 1, RANK_BLOCK))
    return rank.reshape(a), cnt[:n_experts, 0]


INVERT_CHUNK = 4096


def _invert_kernel(dest_ref, lo_ref, hi_ref, out_ref, *, chunk, n_exp, n_slots):
    step = pl.program_id(0)

    def fill_range(lo, hi):
        def fill(s, c):
            out_ref[s] = -1
            return c
        lax.fori_loop(lo, hi, fill, 0)

    @pl.when(step == 0)
    def _():
        for e in range(n_exp):
            fill_range(lo_ref[e], hi_ref[e])
        fill_range(hi_ref[n_exp - 1], n_slots)

    @pl.when(step > 0)
    def _():
        base = (step - 1) * chunk

        def put(i, c):
            a = base + i
            out_ref[dest_ref[a]] = a
            return c
        lax.fori_loop(0, chunk, put, 0, unroll=8)


def _invert_dest(dest, pad_lo, pad_hi, n_slots):
    a = dest.shape[0]
    chunk = min(INVERT_CHUNK, a)
    assert a % chunk == 0
    smem = pl.BlockSpec(memory_space=pltpu.SMEM)
    return pl.pallas_call(
        functools.partial(_invert_kernel, chunk=chunk, n_exp=pad_lo.shape[0], n_slots=n_slots),
        grid=(a // chunk + 1,),
        in_specs=[smem, smem, smem],
        out_specs=smem,
        out_shape=jax.ShapeDtypeStruct((n_slots,), jnp.int32),
        compiler_params=_params("arbitrary"),
        name="moe_invert",
    )(dest, pad_lo, pad_hi)


def _expert_kernel(be_ref, nxt_ref, dst_ref, x_hbm, wgu_hbm, bgu_ref, wdn_hbm, bdn_ref, out_hbm,
                   xbuf0, xbuf1, ybuf0, ybuf1, hbuf, wgu_st, wdn_st, wgu_bf, wdn_bf, gsem, ssem, wsem,
                   *, layer, n_tokens, n_blocks, ff):
    j = pl.program_id(0)
    R = MOE_BLOCK
    xbufs = (xbuf0, xbuf1)
    ybufs = (ybuf0, ybuf1)
    e = be_ref[j]

    def weight_copies(ex):
        return (pltpu.make_async_copy(wgu_hbm.at[layer, ex], wgu_st, wsem.at[0]),
                pltpu.make_async_copy(wdn_hbm.at[layer, ex], wdn_st, wsem.at[1]))

    @pl.when(j == 0)
    def _():
        for cp in weight_copies(e):
            cp.start()

    @pl.when((j == 0) | (e != be_ref[jnp.maximum(j - 1, 0)]))
    def _():
        for cp in weight_copies(e):
            cp.wait()
        wgu_bf[...] = wgu_st[...].astype(BF16)
        wdn_bf[...] = wdn_st[...].astype(BF16)
        nxt = nxt_ref[j]

        @pl.when(nxt >= 0)
        def _():
            for cp in weight_copies(nxt):
                cp.start()

    def start_gather(blk, s):
        for r in range(R):
            tok = dst_ref[blk, r] & (n_tokens - 1)
            pltpu.make_async_copy(x_hbm.at[pl.ds(tok, 1)], xbufs[s].at[pl.ds(r, 1)], gsem.at[s]).start()

    def wait_gather(s):
        pltpu.make_async_copy(x_hbm.at[pl.ds(0, R)], xbufs[s], gsem.at[s]).wait()

    def start_scatter(blk, s):
        for r in range(R):
            pltpu.make_async_copy(ybufs[s].at[pl.ds(r, 1)], out_hbm.at[pl.ds(dst_ref[blk, r], 1)],
                                  ssem.at[s]).start()

    def wait_scatter(s):
        pltpu.make_async_copy(ybufs[s], out_hbm.at[pl.ds(0, R)], ssem.at[s]).wait()

    def up_proj(s):
        xb = xbufs[s][...].astype(BF16)
        hgu = jnp.dot(xb, wgu_bf[...], preferred_element_type=F32) + bgu_ref[e]
        g = jnp.minimum(hgu[:, :ff], SWIGLU_LIMIT)
        u = jnp.clip(hgu[:, ff:], -SWIGLU_LIMIT, SWIGLU_LIMIT)
        hbuf[...] = ((u + 1.0) * (g * jax.nn.sigmoid(g * SWIGLU_ALPHA))).astype(BF16)

    def down_proj(s):
        ybufs[s][...] = jnp.dot(hbuf[...], wdn_bf[...], preferred_element_type=F32) + bdn_ref[e]

    @pl.when(j == 0)
    def _():
        start_gather(0, 0)
        wait_gather(0)
        start_gather(1, 1)
        up_proj(0)
        down_proj(0)

    for s in (0, 1):
        @pl.when((j > 0) & (lax.rem(j, 2) == s))
        def _(s=s):
            wait_gather(s)
            start_gather(jnp.minimum(j + 1, n_blocks - 1), 1 - s)
            up_proj(s)

            @pl.when(j >= 2)
            def _():
                wait_scatter(s)

            start_scatter(j - 1, 1 - s)
            down_proj(s)

    @pl.when(j == n_blocks - 1)
    def _():
        s_last = (n_blocks - 1) % 2
        start_scatter(n_blocks - 1, s_last)
        wait_gather(1 - s_last)
        wait_scatter(1 - s_last)
        wait_scatter(s_last)


def _expert_blocks(x, block_e, next_e, dst_rows, w_gu_all, w_dn_all, layer, b_gu, b_dn):
    t, d = x.shape
    n_blocks = block_e.shape[0]
    _, n_exp, _, ff2 = w_gu_all.shape
    ff = ff2 // 2
    assert t & (t - 1) == 0, "token count must be a power of two"
    assert n_blocks >= 2
    whole = lambda j, be, nx, ds: (0, 0, 0)
    return pl.pallas_call(
        functools.partial(_expert_kernel, layer=layer, n_tokens=t, n_blocks=n_blocks, ff=ff),
        grid_spec=pltpu.PrefetchScalarGridSpec(
            num_scalar_prefetch=3,
            grid=(n_blocks,),
            in_specs=[
                pl.BlockSpec(memory_space=pl.ANY),
                pl.BlockSpec(memory_space=pl.ANY),
                pl.BlockSpec((n_exp, 1, ff2), whole),
                pl.BlockSpec(memory_space=pl.ANY),
                pl.BlockSpec((n_exp, 1, d), whole),
            ],
            out_specs=pl.BlockSpec(memory_space=pl.ANY),
            scratch_shapes=[
                pltpu.VMEM((MOE_BLOCK, d), F32),
                pltpu.VMEM((MOE_BLOCK, d), F32),
                pltpu.VMEM((MOE_BLOCK, d), F32),
                pltpu.VMEM((MOE_BLOCK, d), F32),
                pltpu.VMEM((MOE_BLOCK, ff), BF16),
                pltpu.VMEM((d, ff2), F32),
                pltpu.VMEM((ff, d), F32),
                pltpu.VMEM((d, ff2), BF16),
                pltpu.VMEM((ff, d), BF16),
                pltpu.SemaphoreType.DMA((2,)),
                pltpu.SemaphoreType.DMA((2,)),
                pltpu.SemaphoreType.DMA((2,)),
            ],
        ),
        out_shape=jax.ShapeDtypeStruct((n_blocks * MOE_BLOCK, d), F32),
        compiler_params=_params("arbitrary"),
        name="moe_experts",
    )(block_e, next_e, dst_rows, x, w_gu_all, b_gu.reshape(n_exp, 1, ff2), w_dn_all, b_dn.reshape(n_exp, 1, d))


def _combine_kernel(x_ref, y0_ref, y1_ref, y2_ref, y3_ref, gate_ref, g_ref, b_ref, o_ref, obf_ref, *, alpha):
    gate = gate_ref[...]
    ffn = gate[:, 0:1] * y0_ref[...]
    for k, y_ref in enumerate((y1_ref, y2_ref, y3_ref), start=1):
        ffn = ffn + gate[:, k:k + 1] * y_ref[...]
    y = _layer_norm(alpha * x_ref[...] + ffn, g_ref[...], b_ref[...])
    o_ref[...] = y
    obf_ref[...] = y.astype(BF16)


def _combine_ln(x, ys, gate_pad, g, b, alpha, *, tm=256):
    t, d = x.shape
    tm = min(tm, t)
    nt = t // tm
    row = lambda i: (i, 0)
    fix = lambda i: (0, 0)
    y_specs = [pl.BlockSpec((tm, d), functools.partial(lambda i, k: (i + k * nt, 0), k=k)) for k in range(TOP_K)]
    return pl.pallas_call(
        functools.partial(_combine_kernel, alpha=alpha),
        grid=(nt,),
        in_specs=[pl.BlockSpec((tm, d), row)] + y_specs + [
            pl.BlockSpec((tm, LANES), row),
            pl.BlockSpec((1, d), fix),
            pl.BlockSpec((1, d), fix),
        ],
        out_specs=[pl.BlockSpec((tm, d), row), pl.BlockSpec((tm, d), row)],
        out_shape=[jax.ShapeDtypeStruct((t, d), F32), jax.ShapeDtypeStruct((t, d), BF16)],
        compiler_params=_params("parallel"),
        name="moe_combine_ln",
    )(x, ys, ys, ys, ys, gate_pad, g.reshape(1, d), b.reshape(1, d))


def _moe_layer(x, layer, rw, rb, w_gu_all, b_gu, w_dn_all, b_dn, g, b, alpha):
    t, d = x.shape
    n_exp = rw.shape[1]
    a = t * TOP_K
    gate_pad, idx_pad = _router(x, rw, rb)
    e_flat = idx_pad[:, :TOP_K].reshape(a)
    rank, counts = _expert_ranks(e_flat, n_exp)
    padded = (counts + MOE_BLOCK - 1) // MOE_BLOCK * MOE_BLOCK
    pend = jnp.cumsum(padded)
    pstart = pend - padded
    dest = pstart[e_flat] + rank
    n_slots = a + n_exp * MOE_BLOCK
    n_blocks = n_slots // MOE_BLOCK
    block_start = jnp.arange(n_blocks, dtype=jnp.int32) * MOE_BLOCK
    block_e = jnp.minimum(jnp.sum((block_start[:, None] >= pend[None, :]).astype(jnp.int32), axis=1), n_exp - 1)
    slot_a = _invert_dest(dest, pstart + counts, pend, n_slots)
    valid = slot_a >= 0
    pad_idx = jnp.cumsum(jnp.logical_not(valid).astype(jnp.int32)) - 1
    dst_rows = jnp.where(valid, (slot_a % TOP_K) * t + slot_a // TOP_K, a + pad_idx).astype(jnp.int32)
    run_end = jnp.sum((block_e[None, :] <= block_e[:, None]).astype(jnp.int32), axis=1)
    next_e = jnp.where(run_end < n_blocks, block_e[jnp.minimum(run_end, n_blocks - 1)], -1).astype(jnp.int32)
    ys = _expert_blocks(x, block_e, next_e, dst_rows.reshape(n_blocks, MOE_BLOCK), w_gu_all, w_dn_all, layer,
                        b_gu, b_dn)
    return _combine_ln(x, ys, gate_pad, g, b, alpha)


def _mlstm_layer(x, x_bf, w_in_all, j, b_gates, norm_g, w_out_all, g, b, alpha, *, batch, seq):
    H = MLSTM_HEADS
    n_main = w_in_all.shape[2] - 4 * H
    proj = _matmul(x_bf, w_in_all, j, 0, n_main, jnp.zeros((n_main,), F32))
    gates_t = _mlstm_gates(x_bf, w_in_all[j, :, n_main:], b_gates)
    h = _mlstm_scan(proj, gates_t, norm_g, batch=batch, seq=seq)
    return _matmul_res_ln(h, w_out_all, j, x, g, b, alpha)


def _gmlp_layer(x, x_bf, w_in_all, j, b_in, vn_g, vn_b, w_s, b_s, w_out_all, g, b, alpha):
    width = w_out_all.shape[1]
    u = _matmul(x_bf, w_in_all, j, 0, width, b_in[:width], act="gelu")
    vn = _matmul_gelu_ln(x_bf, w_in_all, j, 1, width, b_in[width:], vn_g, vn_b)
    y = _gmlp_spatial(u, vn, w_s, b_s)
    return _matmul_res_ln(y, w_out_all, j, x, g, b, alpha)


def kernel(x, mlstm_w_in, mlstm_b_gates, mlstm_norm_g, mlstm_w_out, gmlp_w_in, gmlp_b_in, gmlp_vnorm_g,
           gmlp_vnorm_b, gmlp_w_s, gmlp_b_s, gmlp_w_out, router_w, router_b, expert_w_gu, expert_b_gu,
           expert_w_down, expert_b_down, ln_g, ln_b):
    batch, seq, d = x.shape
    depth = ln_g.shape[0]
    alpha = float((2 * depth) ** 0.25)
    xf = x.reshape(batch * seq, d)
    x_bf = xf.astype(BF16)
    for layer in range(depth):
        j = layer // 2
        if layer % 2 == 0:
            xf, x_bf = _mlstm_layer(xf, x_bf, mlstm_w_in, j, mlstm_b_gates[j], mlstm_norm_g[j], mlstm_w_out,
                                    ln_g[layer, 0], ln_b[layer, 0], alpha, batch=batch, seq=seq)
        else:
            xf, x_bf = _gmlp_layer(xf, x_bf, gmlp_w_in, j, gmlp_b_in[j], gmlp_vnorm_g[j], gmlp_vnorm_b[j],
                                   gmlp_w_s[j], gmlp_b_s[j], gmlp_w_out,
                                   ln_g[layer, 0], ln_b[layer, 0], alpha)
        xf, x_bf = _moe_layer(xf, layer, router_w[layer], router_b[layer],
                              expert_w_gu, expert_b_gu[layer], expert_w_down, expert_b_down[layer],
                              ln_g[layer, 1], ln_b[layer, 1], alpha)
    return xf.reshape(batch, seq, d)
```
